```python
import math
import jax, jax.numpy as jnp
from jax import lax
import numpy as np

D_MODEL = 4096
BATCH = 1
SEQ = 8192
DEPTH = 1
DEC_BATCH = 8
DEC_SEQ = 2048
PAST_LEN = 128

HEAD_DIM = 128
N_HEADS_TOTAL = D_MODEL // HEAD_DIM
NA_HEADS = N_HEADS_TOTAL // 2
SWA_Q_HEADS = N_HEADS_TOTAL - NA_HEADS
SWA_KV_HEADS = max(1, SWA_Q_HEADS // 4)
NA_WIDTH = NA_HEADS * HEAD_DIM
SWA_Q_WIDTH = SWA_Q_HEADS * HEAD_DIM
SWA_KV_WIDTH = SWA_KV_HEADS * HEAD_DIM
MIX_WIDTH = NA_WIDTH + SWA_Q_WIDTH
IN_WIDTH = 3 * NA_WIDTH + SWA_Q_WIDTH + 2 * SWA_KV_WIDTH
IN_SPLITS = (NA_WIDTH, 2 * NA_WIDTH, 3 * NA_WIDTH, 3 * NA_WIDTH + SWA_Q_WIDTH,
             3 * NA_WIDTH + SWA_Q_WIDTH + SWA_KV_WIDTH)
D_FF = 4 * D_MODEL
GRID_W = 64
NA_ROWS = 8
NA_COLS = 16
SWA_WINDOW = 128
SWA_BLOCK = 128
T5_BUCKETS = 32
T5_MAX_DIST = 128
N_META = 16
NORM_EPS = 1e-6
NEG_INF = -1e30

kernel_name = "hymba_na_swa_encoder"


def rms_norm(x, gain):
    xf = x.astype(jnp.float32)
    y = xf * lax.rsqrt(jnp.mean(xf * xf, axis=-1, keepdims=True) + NORM_EPS)
    return (y * gain.astype(jnp.float32)).astype(x.dtype)


def t5_bucket(rel):
    nb = T5_BUCKETS // 2
    max_exact = nb // 2
    ret = np.where(rel > 0, nb, 0)
    n = np.abs(rel)
    large = max_exact + (np.log(np.maximum(n, 1) / max_exact)
                         / math.log(T5_MAX_DIST / max_exact) * (nb - max_exact)).astype(np.int64)
    large = np.minimum(large, nb - 1)
    return ret + np.where(n < max_exact, n, large)


def neighborhood_attention(q, k, v, rpb, with_meta_queries):
    batch, L, H, Dh = q.shape
    T = L - N_META
    rows = T // GRID_W
    kr_ = min(NA_ROWS, rows)
    scale = Dh ** -0.5
    qm, km, vm = q[:, :N_META], k[:, :N_META], v[:, :N_META]
    r = np.arange(rows)
    rs = np.clip(r - kr_ // 2, 0, rows - kr_)
    row_idx = rs[:, None] + np.arange(kr_)[None, :]
    dr = row_idx - r[:, None]
    c = np.arange(GRID_W)
    cs = np.clip(c - NA_COLS // 2, 0, GRID_W - NA_COLS)
    dc = c[None, :] - c[:, None]
    col_in = (c[None, :] >= cs[:, None]) & (c[None, :] < cs[:, None] + NA_COLS)
    bias = rpb[:, dr[:, None, :, None] + (NA_ROWS - 1),
               np.clip(dc, -(NA_COLS - 1), NA_COLS - 1)[None, :, None, :] + (NA_COLS - 1)]
    qg = q[:, N_META:].reshape(batch, rows, GRID_W, H, Dh)
    kg = k[:, N_META:].reshape(batch, rows, GRID_W, H, Dh)[:, row_idx]
    vg = v[:, N_META:].reshape(batch, rows, GRID_W, H, Dh)[:, row_idx]
    s = jnp.einsum('brqhd,brikhd->bhrqik', qg, kg,
                   preferred_element_type=jnp.float32) * scale + bias[None].astype(jnp.float32)
    s = jnp.where(col_in[:, None, :], s, NEG_INF).reshape(batch, H, rows, GRID_W, kr_ * GRID_W)
    s_m = jnp.einsum('brqhd,bmhd->bhrqm', qg, km, preferred_element_type=jnp.float32) * scale
    p = jax.nn.softmax(jnp.concatenate([s_m, s], axis=-1), axis=-1).astype(v.dtype)
    p_m = p[..., :N_META]
    p_w = p[..., N_META:].reshape(batch, H, rows, GRID_W, kr_, GRID_W)
    o = (jnp.einsum('bhrqm,bmhd->brqhd', p_m, vm, preferred_element_type=jnp.float32)
         + jnp.einsum('bhrqik,brikhd->brqhd', p_w, vg, preferred_element_type=jnp.float32))
    o_real = o.reshape(batch, T, H * Dh).astype(q.dtype)
    o_meta = None
    if with_meta_queries:
        sm = jnp.einsum('bqhd,bmhd->bhqm', qm, km, preferred_element_type=jnp.float32) * scale
        pm = jax.nn.softmax(sm, axis=-1).astype(v.dtype)
        o_meta = jnp.einsum('bhqm,bmhd->bqhd', pm, vm).reshape(batch, N_META, H * Dh)
    return o_meta, o_real


def window_attention(q, k, v, t5_bias, sink, with_meta_queries):
    batch, L, Hq, Dh = q.shape
    Hkv = k.shape[2]
    G = Hq // Hkv
    T = L - N_META
    blk = SWA_BLOCK
    nb = T // blk
    scale = Dh ** -0.5
    qm, km, vm = q[:, :N_META], k[:, :N_META], v[:, :N_META]
    kr, vr = k[:, N_META:], v[:, N_META:]
    qb = q[:, N_META:].reshape(batch, nb, blk, Hkv, G, Dh)
    pad = ((0, 0), (blk, blk), (0, 0), (0, 0))
    kp, vp = jnp.pad(kr, pad), jnp.pad(vr, pad)

    def band(a):
        return jnp.concatenate([a[:, o * blk:o * blk + T].reshape(batch, nb, blk, Hkv, Dh)
                                for o in range(3)], axis=2)

    kw, vw = band(kp), band(vp)
    qq = np.arange(blk)
    jj = np.arange(3 * blk)
    rel_w = (jj[None, :] - blk) - qq[:, None]
    key_t = np.arange(nb)[:, None] * blk - blk + jj[None, :]
    ok_w = (np.abs(rel_w) <= SWA_WINDOW)[None] & ((key_t >= 0) & (key_t < T))[:, None, :]
    bias_w = t5_bias[t5_bucket(rel_w)].transpose(2, 0, 1).reshape(Hkv, G, blk, 3 * blk)
    rel_m = np.arange(N_META)[None, :] - (N_META + np.arange(T))[:, None]
    bias_m = t5_bias[t5_bucket(rel_m)].reshape(nb, blk, N_META, Hkv, G).transpose(0, 3, 4, 1, 2)
    s_w = jnp.einsum('bnqhgd,bnkhd->bnhgqk', qb, kw,
                     preferred_element_type=jnp.float32) * scale + bias_w.astype(jnp.float32)
    s_w = jnp.where(ok_w[:, None, None], s_w, NEG_INF)
    s_m = jnp.einsum('bnqhgd,bmhd->bnhgqm', qb, km,
                     preferred_element_type=jnp.float32) * scale + bias_m.astype(jnp.float32)
    sink_col = jnp.broadcast_to(sink.astype(jnp.float32).reshape(Hkv, G, 1, 1),
                                s_w.shape[:-1] + (1,))
    p = jax.nn.softmax(jnp.concatenate([s_m, s_w, sink_col], axis=-1), axis=-1).astype(v.dtype)
    o = (jnp.einsum('bnhgqm,bmhd->bnqhgd', p[..., :N_META], vm, preferred_element_type=jnp.float32)
         + jnp.einsum('bnhgqk,bnkhd->bnqhgd', p[..., N_META:N_META + 3 * blk], vw,
                      preferred_element_type=jnp.float32))
    o_real = o.reshape(batch, T, Hq * Dh).astype(q.dtype)
    o_meta = None
    if with_meta_queries:
        kq = jnp.concatenate([km, kr[:, :blk]], axis=1)
        vq = jnp.concatenate([vm, vr[:, :blk]], axis=1)
        kpos = np.arange(N_META + blk)
        rel_q = kpos[None, :] - np.arange(N_META)[:, None]
        ok_q = (kpos[None, :] < N_META) | (np.abs(rel_q) <= SWA_WINDOW)
        bias_q = t5_bias[t5_bucket(rel_q)].transpose(2, 0, 1).reshape(Hkv, G, N_META, N_META + blk)
        qmg = qm.reshape(batch, N_META, Hkv, G, Dh)
        sq = jnp.einsum('bqhgd,bkhd->bhgqk', qmg, kq,
                        preferred_element_type=jnp.float32) * scale + bias_q.astype(jnp.float32)
        sq = jnp.where(ok_q, sq, NEG_INF)
        sink_q = jnp.broadcast_to(sink.astype(jnp.float32).reshape(Hkv, G, 1, 1), sq.shape[:-1] + (1,))
        pq = jax.nn.softmax(jnp.concatenate([sq, sink_q], axis=-1), axis=-1).astype(v.dtype)
        o_meta = jnp.einsum('bhgqk,bkhd->bqhgd', pq[..., :-1], vq).reshape(batch, N_META, Hq * Dh)
    return o_meta, o_real


def sq_relu_mlp(h, gain, w_up, w_down):
    u = rms_norm(h, gain) @ w_up
    return jnp.square(jax.nn.relu(u)) @ w_down


def encoder_layer(m, x, t5_bias, g_attn, w_in, qn_na, kn_na, rpb, qn_sw, kn_sw, sink,
                  w_out, g_mlp, w_up, w_down, update_meta):
    batch = x.shape[0]
    h = jnp.concatenate([m, x], axis=1)
    L = h.shape[1]
    proj = rms_norm(h, g_attn) @ w_in
    q_na, k_na, v_na, q_sw, k_sw, v_sw = jnp.split(proj, IN_SPLITS, axis=-1)

    def heads(a, n):
        return a.reshape(batch, L, n, HEAD_DIM)

    q_na = rms_norm(heads(q_na, NA_HEADS), qn_na)
    k_na = rms_norm(heads(k_na, NA_HEADS), kn_na)
    q_sw = rms_norm(heads(q_sw, SWA_Q_HEADS), qn_sw)
    k_sw = rms_norm(heads(k_sw, SWA_KV_HEADS), kn_sw)
    na_m, na_r = neighborhood_attention(q_na, k_na, heads(v_na, NA_HEADS), rpb, update_meta)
    sw_m, sw_r = window_attention(q_sw, k_sw, heads(v_sw, SWA_KV_HEADS), t5_bias, sink, update_meta)
    x = x + jnp.concatenate([na_r, sw_r], axis=-1) @ w_out
    x = x + sq_relu_mlp(x, g_mlp, w_up, w_down)
    if update_meta:
        m = m + jnp.concatenate([na_m, sw_m], axis=-1) @ w_out
        m = m + sq_relu_mlp(m, g_mlp, w_up, w_down)
    return m, x


def encoder_trunk(x, meta_tokens, t5_bias, norm_attn, w_in, q_norm_na, k_norm_na, na_rpb,
                  q_norm_swa, k_norm_swa, swa_sink, w_out, norm_mlp, w_up, w_down):
    batch = x.shape[0]
    m = jnp.broadcast_to(meta_tokens.astype(x.dtype)[None], (batch, N_META, D_MODEL))
    for layer in range(DEPTH):
        m, x = encoder_layer(m, x, t5_bias, norm_attn[layer], w_in[layer], q_norm_na[layer],
                             k_norm_na[layer], na_rpb[layer], q_norm_swa[layer], k_norm_swa[layer],
                             swa_sink[layer], w_out[layer], norm_mlp[layer], w_up[layer],
                             w_down[layer], update_meta=layer < DEPTH - 1)
    return x


def setup_inputs(seed: int = 0) -> dict:
    key = jax.random.key(seed)
    ks = jax.random.split(key, 16)
    f32 = jnp.float32
    nrm = jax.random.normal
    return {
        "x_prompt": nrm(ks[0], (BATCH, SEQ, D_MODEL), f32),
        "x_sample": nrm(ks[1], (DEC_BATCH, DEC_SEQ, D_MODEL), f32),
        "meta_tokens": nrm(ks[2], (N_META, D_MODEL), f32),
        "t5_bias": 0.3 * nrm(ks[3], (T5_BUCKETS, SWA_Q_HEADS), f32),
        "norm_attn": 1.0 + 0.02 * nrm(ks[4], (DEPTH, D_MODEL), f32),
        "w_in": nrm(ks[5], (DEPTH, D_MODEL, IN_WIDTH), f32) * D_MODEL ** -0.5,
        "q_norm_na": 1.0 + 0.02 * nrm(ks[6], (DEPTH, HEAD_DIM), f32),
        "k_norm_na": 1.0 + 0.02 * nrm(ks[7], (DEPTH, HEAD_DIM), f32),
        "na_rpb": 0.3 * nrm(ks[8], (DEPTH, NA_HEADS, 2 * NA_ROWS - 1, 2 * NA_COLS - 1), f32),
        "q_norm_swa": 1.0 + 0.02 * nrm(ks[9], (DEPTH, HEAD_DIM), f32),
        "k_norm_swa": 1.0 + 0.02 * nrm(ks[10], (DEPTH, HEAD_DIM), f32),
        "swa_sink": 0.5 * nrm(ks[11], (DEPTH, SWA_Q_HEADS), f32),
        "w_out": nrm(ks[12], (DEPTH, MIX_WIDTH, D_MODEL), f32) * MIX_WIDTH ** -0.5,
        "norm_mlp": 1.0 + 0.02 * nrm(ks[13], (DEPTH, D_MODEL), f32),
        "w_up": nrm(ks[14], (DEPTH, D_MODEL, D_FF), f32) * D_MODEL ** -0.5,
        "w_down": nrm(ks[15], (DEPTH, D_FF, D_MODEL), f32) * D_FF ** -0.5,
    }


def reference(x_prompt, x_sample, meta_tokens, t5_bias, norm_attn, w_in, q_norm_na, k_norm_na,
              na_rpb, q_norm_swa, k_norm_swa, swa_sink, w_out, norm_mlp, w_up, w_down):
    y_prompt = encoder_trunk(x_prompt, meta_tokens, t5_bias, norm_attn, w_in, q_norm_na, k_norm_na,
                             na_rpb, q_norm_swa, k_norm_swa, swa_sink, w_out, norm_mlp, w_up, w_down)
    y_sample = encoder_trunk(x_sample, meta_tokens, t5_bias, norm_attn, w_in, q_norm_na, k_norm_na,
                             na_rpb, q_norm_swa, k_norm_swa, swa_sink, w_out, norm_mlp, w_up, w_down)
    return (y_prompt, y_sample)
```

```python
import functools
import math

import jax
import jax.numpy as jnp
import numpy as np
from jax import lax
from jax.experimental import pallas as pl
from jax.experimental.pallas import tpu as pltpu

F32 = jnp.float32
BF16 = jnp.bfloat16

HEAD_DIM = 128
N_META = 16
GRID_W = 64
NA_ROWS = 8
NA_COLS = 16
SWA_WINDOW = 128
SWA_BLOCK = 128
T5_BUCKETS = 32
T5_MAX_DIST = 128
NORM_EPS = 1e-6
NEG_INF = -1e30

LANES = 128
NA_QROWS = 4
NA_QBLK = NA_QROWS * GRID_W
NA_KBLKS = 3
MIB = 1024 * 1024


def _cparams(sem, vmem_mib):
    return pltpu.CompilerParams(dimension_semantics=sem, vmem_limit_bytes=vmem_mib * MIB)


def _rms_rows(x, gain):
    ms = jnp.mean(x * x, axis=-1, keepdims=True)
    return x * lax.rsqrt(ms + NORM_EPS) * gain


def _inproj_kernel(x_ref, g_ref, w_ref, hg_ref, hf_ref, o_ref, xn_ref):
    @pl.when(pl.program_id(1) == 0)
    def _():
        xn_ref[...] = _rms_rows(x_ref[...], g_ref[...]).astype(BF16)

    res = jnp.dot(xn_ref[...], w_ref[...], preferred_element_type=F32)
    for c in range(o_ref.shape[0]):
        blk = res[:, c * LANES:(c + 1) * LANES]
        r = lax.rsqrt(jnp.mean(blk * blk, axis=-1, keepdims=True) + NORM_EPS)
        f = hf_ref[c]
        o_ref[c] = (blk * (f * r + (1.0 - f)) * hg_ref[c]).astype(BF16)


def _inproj(x2, gain, w, head_gain, head_flag, bm, bn):
    m, d = x2.shape
    n = w.shape[1]
    cpb = bn // LANES
    return pl.pallas_call(
        _inproj_kernel,
        grid=(m // bm, n // bn),
        in_specs=[
            pl.BlockSpec((bm, d), lambda i, j: (i, 0)),
            pl.BlockSpec((1, d), lambda i, j: (0, 0)),
            pl.BlockSpec((d, bn), lambda i, j: (0, j)),
            pl.BlockSpec((cpb, 1, LANES), lambda i, j: (j, 0, 0)),
            pl.BlockSpec((cpb, 1, LANES), lambda i, j: (j, 0, 0)),
        ],
        out_specs=pl.BlockSpec((cpb, bm, LANES), lambda i, j: (j, i, 0)),
        out_shape=jax.ShapeDtypeStruct((n // LANES, m, LANES), BF16),
        scratch_shapes=[pltpu.VMEM((bm, d), BF16)],
        compiler_params=_cparams(("parallel", "arbitrary"), 48),
        name="inproj",
    )(x2, gain, w, head_gain, head_flag)


def _mlp_up_kernel(x_ref, g_ref, w_ref, o_ref, xn_ref):
    @pl.when(pl.program_id(1) == 0)
    def _():
        xn_ref[...] = _rms_rows(x_ref[...], g_ref[...]).astype(BF16)

    u = jnp.dot(xn_ref[...], w_ref[...], preferred_element_type=F32)
    u = jnp.maximum(u, 0.0)
    o_ref[...] = (u * u).astype(BF16)


def _mlp_up(x2, gain, w, bm, bn):
    m, d = x2.shape
    n = w.shape[1]
    return pl.pallas_call(
        _mlp_up_kernel,
        grid=(m // bm, n // bn),
        in_specs=[
            pl.BlockSpec((bm, d), lambda i, j: (i, 0)),
            pl.BlockSpec((1, d), lambda i, j: (0, 0)),
            pl.BlockSpec((d, bn), lambda i, j: (0, j)),
        ],
        out_specs=pl.BlockSpec((bm, bn), lambda i, j: (i, j)),
        out_shape=jax.ShapeDtypeStruct((m, n), BF16),
        scratch_shapes=[pltpu.VMEM((bm, d), BF16)],
        compiler_params=_cparams(("parallel", "arbitrary"), 48),
        name="mlp_up",
    )(x2, gain, w)


def _mlp_down_kernel(a_ref, w_ref, r_ref, o_ref, acc_ref):
    k = pl.program_id(2)

    @pl.when(k == 0)
    def _():
        acc_ref[...] = jnp.zeros_like(acc_ref)

    acc_ref[...] += jnp.dot(a_ref[...], w_ref[...], preferred_element_type=F32)

    @pl.when(k == pl.num_programs(2) - 1)
    def _():
        o_ref[...] = r_ref[...] + acc_ref[...]


def _mlp_down(a, w, resid, bm, bn, bk):
    m, kd = a.shape
    n = w.shape[1]
    return pl.pallas_call(
        _mlp_down_kernel,
        grid=(m // bm, n // bn, kd // bk),
        in_specs=[
            pl.BlockSpec((bm, bk), lambda i, j, k: (i, k)),
            pl.BlockSpec((bk, bn), lambda i, j, k: (k, j)),
            pl.BlockSpec((bm, bn), lambda i, j, k: (i, j)),
        ],
        out_specs=pl.BlockSpec((bm, bn), lambda i, j, k: (i, j)),
        out_shape=jax.ShapeDtypeStruct((m, n), F32),
        scratch_shapes=[pltpu.VMEM((bm, bn), F32)],
        compiler_params=_cparams(("parallel", "parallel", "arbitrary"), 48),
        name="mlp_down",
    )(a, w, resid)


def _outproj_kernel(na_ref, sw_ref, wt_ref, wb_ref, r_ref, o_ref):
    acc = jnp.dot(na_ref[...], wt_ref[...], preferred_element_type=F32)
    acc = acc + jnp.dot(sw_ref[...], wb_ref[...], preferred_element_type=F32)
    o_ref[...] = r_ref[...] + acc


def _outproj(na_o, sw_o, w, resid, bm, bn):
    m, kh = na_o.shape
    n = w.shape[1]
    return pl.pallas_call(
        _outproj_kernel,
        grid=(m // bm, n // bn),
        in_specs=[
            pl.BlockSpec((bm, kh), lambda i, j: (i, 0)),
            pl.BlockSpec((bm, kh), lambda i, j: (i, 0)),
            pl.BlockSpec((kh, bn), lambda i, j: (0, j)),
            pl.BlockSpec((kh, bn), lambda i, j: (1, j)),
            pl.BlockSpec((bm, bn), lambda i, j: (i, j)),
        ],
        out_specs=pl.BlockSpec((bm, bn), lambda i, j: (i, j)),
        out_shape=jax.ShapeDtypeStruct((m, n), F32),
        compiler_params=_cparams(("parallel", "parallel"), 48),
        name="outproj",
    )(na_o, sw_o, w, w, resid)


def _qkt(q, k):
    return lax.dot_general(q, k, (((1,), (1,)), ((), ())), preferred_element_type=F32)


def _na_kernel(q_ref, k0_ref, k1_ref, k2_ref, v0_ref, v1_ref, v2_ref, km_ref, vm_ref, tab_ref,
               o_ref, *, scale):
    q = q_ref[...]
    k = jnp.concatenate([k0_ref[...], k1_ref[...], k2_ref[...]], axis=0)
    v = jnp.concatenate([v0_ref[...], v1_ref[...], v2_ref[...]], axis=0)
    s = _qkt(q, k) * scale + tab_ref[...]
    sm = _qkt(q, km_ref[...]) * scale
    mx = jnp.maximum(jnp.max(s, axis=-1, keepdims=True), jnp.max(sm, axis=-1, keepdims=True))
    p = jnp.exp(s - mx)
    pm = jnp.exp(sm - mx)
    den = jnp.sum(p, axis=-1, keepdims=True) + jnp.sum(pm, axis=-1, keepdims=True)
    o = jnp.dot(p.astype(BF16), v, preferred_element_type=F32)
    o = o + jnp.dot(pm.astype(BF16), vm_ref[...], preferred_element_type=F32)
    o_ref[...] = (o / den).astype(BF16)


def _na_attention(proj, km, vm, tab, batch, rows):
    nh = km.shape[0]
    nblk = rows // NA_QROWS
    tokens = proj.shape[1]

    def q_map(h, b, j):
        return (h, b * nblk + j, 0)

    def kv_map(base, d):
        def f(h, b, j):
            ws = jnp.clip(j - 1, 0, nblk - NA_KBLKS)
            return (base + h, b * nblk + ws + d, 0)
        return f

    def tab_map(h, b, j):
        variant = jnp.where(j == 0, 0, jnp.where(j == nblk - 1, 2, 1))
        return (variant, h, 0, 0)

    blk = pl.BlockSpec((None, NA_QBLK, LANES), q_map)
    kspecs = [pl.BlockSpec((None, NA_QBLK, LANES), kv_map(nh, d)) for d in range(NA_KBLKS)]
    vspecs = [pl.BlockSpec((None, NA_QBLK, LANES), kv_map(2 * nh, d)) for d in range(NA_KBLKS)]
    mspec = pl.BlockSpec((None, N_META, LANES), lambda h, b, j: (h, 0, 0))
    return pl.pallas_call(
        functools.partial(_na_kernel, scale=HEAD_DIM ** -0.5),
        grid=(nh, batch, nblk),
        in_specs=[blk] + kspecs + vspecs + [mspec, mspec,
                  pl.BlockSpec((None, None, NA_QBLK, NA_KBLKS * NA_QBLK), tab_map)],
        out_specs=pl.BlockSpec((NA_QBLK, LANES), lambda h, b, j: (b * nblk + j, h)),
        out_shape=jax.ShapeDtypeStruct((tokens, nh * LANES), BF16),
        compiler_params=_cparams(("parallel", "parallel", "arbitrary"), 32),
        name="na_attn",
    )(proj, proj, proj, proj, proj, proj, proj, km, vm, tab)


def _swa_kernel(sink_ref, q_ref, k0_ref, k1_ref, k2_ref, v0_ref, v1_ref, v2_ref, km_ref, vm_ref,
                bw_ref, bm_ref, o_ref, *, scale, group):
    hk = pl.program_id(0)
    i = pl.program_id(2)
    nb = pl.num_programs(2)
    blk = q_ref.shape[1]
    q = q_ref[...].reshape(group * blk, LANES)
    k = jnp.concatenate([k0_ref[...], k1_ref[...], k2_ref[...]], axis=0)
    v = jnp.concatenate([v0_ref[...], v1_ref[...], v2_ref[...]], axis=0)
    s = _qkt(q, k) * scale + bw_ref[...].reshape(group * blk, 3 * blk)
    lane = lax.broadcasted_iota(jnp.int32, s.shape, 1)
    outside = ((i == 0) & (lane < blk)) | ((i == nb - 1) & (lane >= 2 * blk))
    s = jnp.where(outside, NEG_INF, s)
    sm = _qkt(q, km_ref[...]) * scale + bm_ref[...].reshape(group * blk, N_META)
    row = lax.broadcasted_iota(jnp.int32, (group * blk, 1), 0)
    sink = jnp.zeros((group * blk, 1), F32)
    for g in range(group):
        sink = jnp.where(row // blk == g, sink_ref[hk * group + g], sink)
    mx = jnp.maximum(jnp.max(s, axis=-1, keepdims=True), jnp.max(sm, axis=-1, keepdims=True))
    mx = jnp.maximum(mx, sink)
    p = jnp.exp(s - mx)
    pm = jnp.exp(sm - mx)
    den = (jnp.sum(p, axis=-1, keepdims=True) + jnp.sum(pm, axis=-1, keepdims=True)
           + jnp.exp(sink - mx))
    o = jnp.dot(p.astype(BF16), v, preferred_element_type=F32)
    o = o + jnp.dot(pm.astype(BF16), vm_ref[...], preferred_element_type=F32)
    o = (o / den).astype(BF16)
    for g in range(group):
        o_ref[:, g * LANES:(g + 1) * LANES] = o[g * blk:(g + 1) * blk]


def _swa_attention(proj, km, vm, bias_w, bias_m, sink, batch, seq, q_base, k_base, v_base):
    hkv = km.shape[0]
    hq = bias_w.shape[0]
    group = hq // hkv
    blk = SWA_BLOCK
    nb = seq // blk
    tokens = proj.shape[1]

    def kv_map(base, d):
        def f(hk, b, i):
            return (base + hk, b * nb + jnp.clip(i - 1 + d, 0, nb - 1), 0)
        return f

    kspecs = [pl.BlockSpec((None, blk, LANES), kv_map(k_base, d)) for d in range(3)]
    vspecs = [pl.BlockSpec((None, blk, LANES), kv_map(v_base, d)) for d in range(3)]
    mspec = pl.BlockSpec((None, N_META, LANES), lambda hk, b, i: (hk, 0, 0))
    return pl.pallas_call(
        functools.partial(_swa_kernel, scale=HEAD_DIM ** -0.5, group=group),
        grid=(hkv, batch, nb),
        in_specs=[pl.BlockSpec(memory_space=pltpu.SMEM),
                  pl.BlockSpec((group, blk, LANES),
                               lambda hk, b, i: (q_base // group + hk, b * nb + i, 0))]
                 + kspecs + vspecs + [mspec, mspec,
                  pl.BlockSpec((group, blk, 3 * blk), lambda hk, b, i: (hk, 0, 0)),
                  pl.BlockSpec((None, group, blk, N_META),
                               lambda hk, b, i: (jnp.minimum(i, 1), hk, 0, 0))],
        out_specs=pl.BlockSpec((blk, group * LANES), lambda hk, b, i: (b * nb + i, hk)),
        out_shape=jax.ShapeDtypeStruct((tokens, hq * LANES), BF16),
        compiler_params=_cparams(("parallel", "parallel", "arbitrary"), 32),
        name="swa_attn",
    )(sink, proj, proj, proj, proj, proj, proj, proj, km, vm, bias_w, bias_m)


def _t5_bucket(rel):
    nb = T5_BUCKETS // 2
    max_exact = nb // 2
    ret = np.where(rel > 0, nb, 0)
    n = np.abs(rel)
    large = max_exact + (np.log(np.maximum(n, 1) / max_exact)
                         / math.log(T5_MAX_DIST / max_exact) * (nb - max_exact)).astype(np.int64)
    large = np.minimum(large, nb - 1)
    return ret + np.where(n < max_exact, n, large)


def _na_tables(rpb):
    vblk = 6
    vrows = vblk * NA_QROWS
    c = np.arange(GRID_W)
    cs = np.clip(c - NA_COLS // 2, 0, GRID_W - NA_COLS)
    col_in = (c[None, :] >= cs[:, None]) & (c[None, :] < cs[:, None] + NA_COLS)
    dc = np.clip(c[None, :] - c[:, None], -(NA_COLS - 1), NA_COLS - 1) + (NA_COLS - 1)
    tabs = []
    for j in (0, 2, vblk - 1):
        qr = NA_QROWS * j + np.arange(NA_QROWS)
        ws = np.clip(j - 1, 0, vblk - NA_KBLKS)
        kr = NA_QROWS * ws + np.arange(NA_KBLKS * NA_QROWS)
        rs = np.clip(qr - NA_ROWS // 2, 0, vrows - NA_ROWS)
        row_ok = (kr[None, :] >= rs[:, None]) & (kr[None, :] < rs[:, None] + NA_ROWS)
        dr = np.clip(kr[None, :] - qr[:, None], -(NA_ROWS - 1), NA_ROWS - 1) + (NA_ROWS - 1)
        bias = rpb[:, dr[:, None, :, None], dc[None, :, None, :]]
        ok = row_ok[:, None, :, None] & col_in[None, :, None, :]
        bias = jnp.where(ok[None], bias.astype(F32), NEG_INF)
        tabs.append(bias.reshape(rpb.shape[0], NA_QBLK, NA_KBLKS * NA_QBLK))
    return jnp.stack(tabs)


def _swa_tables(t5_bias):
    blk = SWA_BLOCK
    qq = np.arange(blk)
    jj = np.arange(3 * blk)
    rel_w = (jj[None, :] - blk) - qq[:, None]
    ok_w = np.abs(rel_w) <= SWA_WINDOW
    bias_w = jnp.where(ok_w[None], t5_bias[_t5_bucket(rel_w)].transpose(2, 0, 1).astype(F32),
                       NEG_INF)
    assert N_META + blk - (N_META - 1) > T5_MAX_DIST
    rel_m = np.arange(N_META)[None, :] - (N_META + np.arange(2 * blk))[:, None]
    bias_m = t5_bias[_t5_bucket(rel_m)].astype(F32)
    bias_m = bias_m.reshape(2, blk, N_META, -1).transpose(0, 3, 1, 2)
    return bias_w, bias_m


def _trunk(x, meta_kv, tabs, params):
    batch, seq, d = x.shape
    (g_attn, w_in, head_gain, head_flag, sink, w_out, g_mlp, w_up, w_down) = params
    km_na, vm_na, km_sw, vm_sw = meta_kv
    na_tab, bias_w, bias_m = tabs
    nh_na = km_na.shape[0]
    hq_sw = bias_w.shape[0]
    hkv_sw = km_sw.shape[0]
    x2 = x.reshape(batch * seq, d)
    proj = _inproj(x2, g_attn, w_in, head_gain, head_flag, 512, 512)
    rows = seq // GRID_W
    assert seq % (GRID_W * NA_QROWS) == 0 and rows // NA_QROWS >= NA_KBLKS and rows >= NA_ROWS
    assert seq % SWA_BLOCK == 0
    na_o = _na_attention(proj, km_na, vm_na, na_tab, batch, rows)
    q_base = 3 * nh_na
    k_base = q_base + hq_sw
    v_base = k_base + hkv_sw
    sw_o = _swa_attention(proj, km_sw, vm_sw, bias_w, bias_m, sink, batch, seq,
                          q_base, k_base, v_base)
    x1 = _outproj(na_o, sw_o, w_out, x2, 512, 1024)
    u = _mlp_up(x1, g_mlp, w_up, 512, 512)
    y = _mlp_down(u, w_down, x1, 1024, 1024, 2048)
    return y.reshape(batch, seq, d)


def kernel(x_prompt, x_sample, meta_tokens, t5_bias, norm_attn, w_in, q_norm_na, k_norm_na, na_rpb,
           q_norm_swa, k_norm_swa, swa_sink, w_out, norm_mlp, w_up, w_down):
    depth = w_in.shape[0]
    assert depth == 1
    d = w_in.shape[1]
    nh_na = na_rpb.shape[1]
    hq_sw = swa_sink.shape[1]
    in_width = w_in.shape[2]
    hkv_sw = (in_width // HEAD_DIM - 3 * nh_na - hq_sw) // 2

    ones = jnp.ones((HEAD_DIM,), F32)

    def rep(g, n):
        return jnp.broadcast_to(g.astype(F32)[None], (n, HEAD_DIM))

    head_gain = jnp.concatenate([
        rep(q_norm_na[0], nh_na), rep(k_norm_na[0], nh_na), rep(ones, nh_na),
        rep(q_norm_swa[0], hq_sw), rep(k_norm_swa[0], hkv_sw), rep(ones, hkv_sw)])[:, None, :]
    flag = np.concatenate([np.ones(2 * nh_na), np.zeros(nh_na), np.ones(hq_sw + hkv_sw),
                           np.zeros(hkv_sw)]).astype(np.float32)
    head_flag = jnp.asarray(np.broadcast_to(flag[:, None, None], (flag.shape[0], 1, HEAD_DIM)))

    w_in_b = w_in[0].astype(BF16)
    w_out_b = w_out[0].astype(BF16)
    w_up_b = w_up[0].astype(BF16)
    w_down_b = w_down[0].astype(BF16)
    g_attn = norm_attn[0].astype(F32)[None]
    g_mlp = norm_mlp[0].astype(F32)[None]
    sink = swa_sink[0].astype(F32)

    mproj = _inproj(meta_tokens.astype(F32), g_attn, w_in_b, head_gain, head_flag, N_META, 512)
    k0 = 3 * nh_na + hq_sw
    meta_kv = (mproj[nh_na:2 * nh_na], mproj[2 * nh_na:3 * nh_na],
               mproj[k0:k0 + hkv_sw], mproj[k0 + hkv_sw:k0 + 2 * hkv_sw])

    tabs = (_na_tables(na_rpb[0]),) + _swa_tables(t5_bias)
    params = (g_attn, w_in_b, head_gain, head_flag, sink, w_out_b, g_mlp, w_up_b, w_down_b)
    y_prompt = _trunk(x_prompt, meta_kv, tabs, params)
    y_sample = _trunk(x_sample, meta_kv, tabs, params)
    return (y_prompt, y_sample)
```

```python
import functools
import math

import jax
import jax.numpy as jnp
import numpy as np
from jax import lax
from jax.experimental import pallas as pl
from jax.experimental.pallas import tpu as pltpu

F32 = jnp.float32
BF16 = jnp.bfloat16

HEAD_DIM = 128
N_META = 16
GRID_W = 64
NA_ROWS = 8
NA_COLS = 16
SWA_WINDOW = 128
SWA_BLOCK = 128
T5_BUCKETS = 32
T5_MAX_DIST = 128
NORM_EPS = 1e-6
NEG_INF = -1e30

LANES = 128
NA_QROWS = 4
NA_QBLK = NA_QROWS * GRID_W
NA_KBLKS = 3
NA_HEADS_PER_STEP = 4
NA_TILES_PER_ROW = NA_KBLKS * NA_QBLK // LANES
MIB = 1024 * 1024

assert 2 * GRID_W == LANES


def _cparams(sem, vmem_mib):
    return pltpu.CompilerParams(dimension_semantics=sem, vmem_limit_bytes=vmem_mib * MIB)


def _rms_rows(x, gain):
    ms = jnp.mean(x * x, axis=-1, keepdims=True)
    return x * lax.rsqrt(ms + NORM_EPS) * gain


def _inproj_kernel(x_ref, g_ref, w_ref, hg_ref, hf_ref, o_ref, xn_ref):
    @pl.when(pl.program_id(1) == 0)
    def _():
        xn_ref[...] = _rms_rows(x_ref[...], g_ref[...]).astype(BF16)

    res = jnp.dot(xn_ref[...], w_ref[...], preferred_element_type=F32)
    for c in range(o_ref.shape[0]):
        blk = res[:, c * LANES:(c + 1) * LANES]
        r = lax.rsqrt(jnp.mean(blk * blk, axis=-1, keepdims=True) + NORM_EPS)
        f = hf_ref[c]
        o_ref[c] = (blk * (f * r + (1.0 - f)) * hg_ref[c]).astype(BF16)


def _inproj(x2, gain, w, head_gain, head_flag, bm, bn):
    m, d = x2.shape
    n = w.shape[1]
    cpb = bn // LANES
    return pl.pallas_call(
        _inproj_kernel,
        grid=(m // bm, n // bn),
        in_specs=[
            pl.BlockSpec((bm, d), lambda i, j: (i, 0)),
            pl.BlockSpec((1, d), lambda i, j: (0, 0)),
            pl.BlockSpec((d, bn), lambda i, j: (0, j)),
            pl.BlockSpec((cpb, 1, LANES), lambda i, j: (j, 0, 0)),
            pl.BlockSpec((cpb, 1, LANES), lambda i, j: (j, 0, 0)),
        ],
        out_specs=pl.BlockSpec((cpb, bm, LANES), lambda i, j: (j, i, 0)),
        out_shape=jax.ShapeDtypeStruct((n // LANES, m, LANES), BF16),
        scratch_shapes=[pltpu.VMEM((bm, d), BF16)],
        compiler_params=_cparams(("parallel", "arbitrary"), 48),
        name="inproj",
    )(x2, gain, w, head_gain, head_flag)


def _mlp_up_kernel(x_ref, g_ref, w_ref, o_ref, xn_ref):
    @pl.when(pl.program_id(1) == 0)
    def _():
        xn_ref[...] = _rms_rows(x_ref[...], g_ref[...]).astype(BF16)

    u = jnp.dot(xn_ref[...], w_ref[...], preferred_element_type=F32)
    u = jnp.maximum(u, 0.0)
    o_ref[...] = (u * u).astype(BF16)


def _mlp_up(x2, gain, w, bm, bn):
    m, d = x2.shape
    n = w.shape[1]
    return pl.pallas_call(
        _mlp_up_kernel,
        grid=(m // bm, n // bn),
        in_specs=[
            pl.BlockSpec((bm, d), lambda i, j: (i, 0)),
            pl.BlockSpec((1, d), lambda i, j: (0, 0)),
            pl.BlockSpec((d, bn), lambda i, j: (0, j)),
        ],
        out_specs=pl.BlockSpec((bm, bn), lambda i, j: (i, j)),
        out_shape=jax.ShapeDtypeStruct((m, n), BF16),
        scratch_shapes=[pltpu.VMEM((bm, d), BF16)],
        compiler_params=_cparams(("parallel", "arbitrary"), 48),
        name="mlp_up",
    )(x2, gain, w)


def _mlp_down_kernel(a_ref, w_ref, r_ref, o_ref, acc_ref):
    k = pl.program_id(2)

    @pl.when(k == 0)
    def _():
        acc_ref[...] = jnp.zeros_like(acc_ref)

    acc_ref[...] += jnp.dot(a_ref[...], w_ref[...], preferred_element_type=F32)

    @pl.when(k == pl.num_programs(2) - 1)
    def _():
        o_ref[...] = r_ref[...] + acc_ref[...]


def _mlp_down(a, w, resid, bm, bn, bk):
    m, kd = a.shape
    n = w.shape[1]
    return pl.pallas_call(
        _mlp_down_kernel,
        grid=(m // bm, n // bn, kd // bk),
        in_specs=[
            pl.BlockSpec((bm, bk), lambda i, j, k: (i, k)),
            pl.BlockSpec((bk, bn), lambda i, j, k: (k, j)),
            pl.BlockSpec((bm, bn), lambda i, j, k: (i, j)),
        ],
        out_specs=pl.BlockSpec((bm, bn), lambda i, j, k: (i, j)),
        out_shape=jax.ShapeDtypeStruct((m, n), F32),
        scratch_shapes=[pltpu.VMEM((bm, bn), F32)],
        compiler_params=_cparams(("parallel", "parallel", "arbitrary"), 48),
        name="mlp_down",
    )(a, w, resid)


def _outproj_kernel(na_ref, sw_ref, wt_ref, wb_ref, r_ref, o_ref):
    acc = jnp.dot(na_ref[...], wt_ref[...], preferred_element_type=F32)
    acc = acc + jnp.dot(sw_ref[...], wb_ref[...], preferred_element_type=F32)
    o_ref[...] = r_ref[...] + acc


def _outproj(na_o, sw_o, w, resid, bm, bn):
    m, kh = na_o.shape
    n = w.shape[1]
    return pl.pallas_call(
        _outproj_kernel,
        grid=(m // bm, n // bn),
        in_specs=[
            pl.BlockSpec((bm, kh), lambda i, j: (i, 0)),
            pl.BlockSpec((bm, kh), lambda i, j: (i, 0)),
            pl.BlockSpec((kh, bn), lambda i, j: (0, j)),
            pl.BlockSpec((kh, bn), lambda i, j: (1, j)),
            pl.BlockSpec((bm, bn), lambda i, j: (i, j)),
        ],
        out_specs=pl.BlockSpec((bm, bn), lambda i, j: (i, j)),
        out_shape=jax.ShapeDtypeStruct((m, n), F32),
        compiler_params=_cparams(("parallel", "parallel"), 48),
        name="outproj",
    )(na_o, sw_o, w, w, resid)


def _qkt(q, k):
    return lax.dot_general(q, k, (((1,), (1,)), ((), ())), preferred_element_type=F32)


def _na_plan():
    vblk = 6
    vrows = vblk * NA_QROWS
    specs, plans = [], []
    for j in (0, 2, vblk - 1):
        ws = int(np.clip(j - 1, 0, vblk - NA_KBLKS))
        plan = []
        for qrl in range(NA_QROWS):
            qr = NA_QROWS * j + qrl
            rs = int(np.clip(qr - NA_ROWS // 2, 0, vrows - NA_ROWS))
            row = []
            for t in range(NA_TILES_PER_ROW):
                kr0 = NA_QROWS * ws + 2 * t
                halves = tuple(kr - qr if rs <= kr < rs + NA_ROWS else None for kr in (kr0, kr0 + 1))
                if halves == (None, None):
                    row.append(-1)
                else:
                    if halves not in specs:
                        specs.append(halves)
                    row.append(specs.index(halves))
            plan.append(row)
        plans.append(plan)
    return specs, plans


def _na_tiles(rpb, specs):
    nh = rpb.shape[0]
    c = np.arange(GRID_W)
    cs = np.clip(c - NA_COLS // 2, 0, GRID_W - NA_COLS)
    col_in = (c[None, :] >= cs[:, None]) & (c[None, :] < cs[:, None] + NA_COLS)
    dc = np.clip(c[None, :] - c[:, None], -(NA_COLS - 1), NA_COLS - 1) + (NA_COLS - 1)
    onehot = np.zeros((2 * NA_COLS - 1, GRID_W * GRID_W), np.float32)
    onehot[dc.reshape(-1), np.arange(GRID_W * GRID_W)] = 1.0
    tcol = jnp.einsum("hab,bx->hax", rpb.astype(F32), onehot, precision=lax.Precision.HIGHEST)
    tcol = jnp.where(col_in[None, None], tcol.reshape(nh, 2 * NA_ROWS - 1, GRID_W, GRID_W), NEG_INF)
    masked = jnp.full((nh, GRID_W, GRID_W), NEG_INF, F32)

    def half(dr):
        return masked if dr is None else tcol[:, dr + NA_ROWS - 1]

    return jnp.stack([jnp.concatenate([half(l), half(r)], axis=-1) for l, r in specs], axis=1)


def _na_kernel(q_ref, k0_ref, k1_ref, k2_ref, v0_ref, v1_ref, v2_ref, km_ref, vm_ref, tiles_ref,
               o_ref, bias_ref, *, scale, plans):
    hb = q_ref.shape[0]
    j = pl.program_id(2)
    last = pl.num_programs(2) - 1

    def build(plan):
        def body():
            for hh in range(hb):
                for qrl in range(NA_QROWS):
                    for t in range(NA_TILES_PER_ROW):
                        idx = plan[qrl][t]
                        tile = (jnp.full((GRID_W, LANES), NEG_INF, F32) if idx < 0
                                else tiles_ref[hh, idx])
                        bias_ref[hh, qrl * GRID_W:(qrl + 1) * GRID_W, t * LANES:(t + 1) * LANES] = tile
        return body

    pl.when(j == 0)(build(plans[0]))
    pl.when(j == 1)(build(plans[1]))
    pl.when(j == last)(build(plans[2]))

    for hh in range(hb):
        q = q_ref[hh]
        k = jnp.concatenate([k0_ref[hh], k1_ref[hh], k2_ref[hh]], axis=0)
        v = jnp.concatenate([v0_ref[hh], v1_ref[hh], v2_ref[hh]], axis=0)
        s = _qkt(q, k) * scale + bias_ref[hh]
        sm = _qkt(q, km_ref[hh]) * scale
        mx = jnp.maximum(jnp.max(s, axis=-1, keepdims=True), jnp.max(sm, axis=-1, keepdims=True))
        p = jnp.exp(s - mx)
        pm = jnp.exp(sm - mx)
        den = jnp.sum(p, axis=-1, keepdims=True) + jnp.sum(pm, axis=-1, keepdims=True)
        o = jnp.dot(p.astype(BF16), v, preferred_element_type=F32)
        o = o + jnp.dot(pm.astype(BF16), vm_ref[hh], preferred_element_type=F32)
        o_ref[:, hh * LANES:(hh + 1) * LANES] = (o / den).astype(BF16)


def _na_attention(proj, km, vm, tiles, plans, batch, rows):
    nh = km.shape[0]
    hb = NA_HEADS_PER_STEP
    nblk = rows // NA_QROWS
    tokens = proj.shape[1]
    assert nblk >= NA_KBLKS and nh % hb == 0
    hblks = nh // hb

    def kv_map(base, d):
        def f(h, b, j):
            ws = jnp.clip(j - 1, 0, nblk - NA_KBLKS)
            return (base + h, b * nblk + ws + d, 0)
        return f

    qspec = pl.BlockSpec((hb, NA_QBLK, LANES), lambda h, b, j: (h, b * nblk + j, 0))
    kspecs = [pl.BlockSpec((hb, NA_QBLK, LANES), kv_map(hblks, d)) for d in range(NA_KBLKS)]
    vspecs = [pl.BlockSpec((hb, NA_QBLK, LANES), kv_map(2 * hblks, d)) for d in range(NA_KBLKS)]
    mspec = pl.BlockSpec((hb, N_META, LANES), lambda h, b, j: (h, 0, 0))
    tspec = pl.BlockSpec((hb,) + tiles.shape[1:], lambda h, b, j: (h, 0, 0, 0))
    return pl.pallas_call(
        functools.partial(_na_kernel, scale=HEAD_DIM ** -0.5, plans=plans),
        grid=(hblks, batch, nblk),
        in_specs=[qspec] + kspecs + vspecs + [mspec, mspec, tspec],
        out_specs=pl.BlockSpec((NA_QBLK, hb * LANES), lambda h, b, j: (b * nblk + j, h)),
        out_shape=jax.ShapeDtypeStruct((tokens, nh * LANES), BF16),
        scratch_shapes=[pltpu.VMEM((hb, NA_QBLK, NA_KBLKS * NA_QBLK), F32)],
        compiler_params=_cparams(("arbitrary", "arbitrary", "arbitrary"), 32),
        name="na_attn",
    )(proj, proj, proj, proj, proj, proj, proj, km, vm, tiles)


def _swa_kernel(sink_ref, q_ref, k0_ref, k1_ref, k2_ref, v0_ref, v1_ref, v2_ref, km_ref, vm_ref,
                bw_ref, bm_ref, o_ref, *, scale, group):
    hk = pl.program_id(0)
    i = pl.program_id(2)
    nb = pl.num_programs(2)
    blk = q_ref.shape[1]
    q = q_ref[...].reshape(group * blk, LANES)
    k = jnp.concatenate([k0_ref[...], k1_ref[...], k2_ref[...]], axis=0)
    v = jnp.concatenate([v0_ref[...], v1_ref[...], v2_ref[...]], axis=0)
    s = _qkt(q, k) * scale + bw_ref[...].reshape(group * blk, 3 * blk)
    lane = lax.broadcasted_iota(jnp.int32, s.shape, 1)
    outside = ((i == 0) & (lane < blk)) | ((i == nb - 1) & (lane >= 2 * blk))
    s = jnp.where(outside, NEG_INF, s)
    sm = _qkt(q, km_ref[...]) * scale + bm_ref[...].reshape(group * blk, N_META)
    row = lax.broadcasted_iota(jnp.int32, (group * blk, 1), 0)
    sink = jnp.zeros((group * blk, 1), F32)
    for g in range(group):
        sink = jnp.where(row // blk == g, sink_ref[hk * group + g], sink)
    mx = jnp.maximum(jnp.max(s, axis=-1, keepdims=True), jnp.max(sm, axis=-1, keepdims=True))
    mx = jnp.maximum(mx, sink)
    p = jnp.exp(s - mx)
    pm = jnp.exp(sm - mx)
    den = (jnp.sum(p, axis=-1, keepdims=True) + jnp.sum(pm, axis=-1, keepdims=True)
           + jnp.exp(sink - mx))
    o = jnp.dot(p.astype(BF16), v, preferred_element_type=F32)
    o = o + jnp.dot(pm.astype(BF16), vm_ref[...], preferred_element_type=F32)
    o = (o / den).astype(BF16)
    for g in range(group):
        o_ref[:, g * LANES:(g + 1) * LANES] = o[g * blk:(g + 1) * blk]


def _swa_attention(proj, km, vm, bias_w, bias_m, sink, batch, seq, q_base, k_base, v_base):
    hkv = km.shape[0]
    hq = bias_w.shape[0]
    group = hq // hkv
    blk = SWA_BLOCK
    nb = seq // blk
    tokens = proj.shape[1]

    def kv_map(base, d):
        def f(hk, b, i):
            return (base + hk, b * nb + jnp.clip(i - 1 + d, 0, nb - 1), 0)
        return f

    kspecs = [pl.BlockSpec((None, blk, LANES), kv_map(k_base, d)) for d in range(3)]
    vspecs = [pl.BlockSpec((None, blk, LANES), kv_map(v_base, d)) for d in range(3)]
    mspec = pl.BlockSpec((None, N_META, LANES), lambda hk, b, i: (hk, 0, 0))
    return pl.pallas_call(
        functools.partial(_swa_kernel, scale=HEAD_DIM ** -0.5, group=group),
        grid=(hkv, batch, nb),
        in_specs=[pl.BlockSpec(memory_space=pltpu.SMEM),
                  pl.BlockSpec((group, blk, LANES),
                               lambda hk, b, i: (q_base // group + hk, b * nb + i, 0))]
                 + kspecs + vspecs + [mspec, mspec,
                  pl.BlockSpec((group, blk, 3 * blk), lambda hk, b, i: (hk, 0, 0)),
                  pl.BlockSpec((None, group, blk, N_META),
                               lambda hk, b, i: (jnp.minimum(i, 1), hk, 0, 0))],
        out_specs=pl.BlockSpec((blk, group * LANES), lambda hk, b, i: (b * nb + i, hk)),
        out_shape=jax.ShapeDtypeStruct((tokens, hq * LANES), BF16),
        compiler_params=_cparams(("parallel", "parallel", "arbitrary"), 32),
        name="swa_attn",
    )(sink, proj, proj, proj, proj, proj, proj, proj, km, vm, bias_w, bias_m)


def _t5_bucket(rel):
    nb = T5_BUCKETS // 2
    max_exact = nb // 2
    ret = np.where(rel > 0, nb, 0)
    n = np.abs(rel)
    large = max_exact + (np.log(np.maximum(n, 1) / max_exact)
                         / math.log(T5_MAX_DIST / max_exact) * (nb - max_exact)).astype(np.int64)
    large = np.minimum(large, nb - 1)
    return ret + np.where(n < max_exact, n, large)


def _select_rows(table, idx):
    flat = idx.reshape(-1)
    onehot = np.zeros((table.shape[0], flat.size), np.float32)
    onehot[flat, np.arange(flat.size)] = 1.0
    out = jnp.einsum("bh,bx->xh", table.astype(F32), onehot, precision=lax.Precision.HIGHEST)
    return out.reshape(idx.shape + (table.shape[1],))


def _swa_tables(t5_bias):
    blk = SWA_BLOCK
    rel = np.arange(-(2 * blk - 1), 2 * blk)
    by_rel = _select_rows(t5_bias, _t5_bucket(rel)).T
    by_rel = jnp.where((np.abs(rel) <= SWA_WINDOW)[None], by_rel, NEG_INF)
    bias_w = jnp.stack([by_rel[:, blk - 1 - q:blk - 1 - q + 3 * blk] for q in range(blk)], axis=1)
    assert N_META + blk - (N_META - 1) > T5_MAX_DIST
    rel_m = np.arange(N_META)[None, :] - (N_META + np.arange(2 * blk))[:, None]
    bias_m = _select_rows(t5_bias, _t5_bucket(rel_m))
    bias_m = bias_m.reshape(2, blk, N_META, -1).transpose(0, 3, 1, 2)
    return bias_w, bias_m


def _trunk(x, meta_kv, tabs, params):
    batch, seq, d = x.shape
    (g_attn, w_in, head_gain, head_flag, sink, w_out, g_mlp, w_up, w_down) = params
    km_na, vm_na, km_sw, vm_sw = meta_kv
    na_tiles, na_plans, bias_w, bias_m = tabs
    nh_na = km_na.shape[0]
    hq_sw = bias_w.shape[0]
    hkv_sw = km_sw.shape[0]
    x2 = x.reshape(batch * seq, d)
    proj = _inproj(x2, g_attn, w_in, head_gain, head_flag, 512, 512)
    rows = seq // GRID_W
    assert seq % (GRID_W * NA_QROWS) == 0 and rows >= NA_ROWS
    assert seq % SWA_BLOCK == 0
    na_o = _na_attention(proj, km_na, vm_na, na_tiles, na_plans, batch, rows)
    q_base = 3 * nh_na
    k_base = q_base + hq_sw
    v_base = k_base + hkv_sw
    sw_o = _swa_attention(proj, km_sw, vm_sw, bias_w, bias_m, sink, batch, seq,
                          q_base, k_base, v_base)
    x1 = _outproj(na_o, sw_o, w_out, x2, 512, 1024)
    u = _mlp_up(x1, g_mlp, w_up, 512, 512)
    y = _mlp_down(u, w_down, x1, 1024, 1024, 2048)
    return y.reshape(batch, seq, d)


def kernel(x_prompt, x_sample, meta_tokens, t5_bias, norm_attn, w_in, q_norm_na, k_norm_na, na_rpb,
           q_norm_swa, k_norm_swa, swa_sink, w_out, norm_mlp, w_up, w_down):
    depth = w_in.shape[0]
    assert depth == 1
    nh_na = na_rpb.shape[1]
    hq_sw = swa_sink.shape[1]
    in_width = w_in.shape[2]
    hkv_sw = (in_width // HEAD_DIM - 3 * nh_na - hq_sw) // 2

    ones = jnp.ones((HEAD_DIM,), F32)

    def rep(g, n):
        return jnp.broadcast_to(g.astype(F32)[None], (n, HEAD_DIM))

    head_gain = jnp.concatenate([
        rep(q_norm_na[0], nh_na), rep(k_norm_na[0], nh_na), rep(ones, nh_na),
        rep(q_norm_swa[0], hq_sw), rep(k_norm_swa[0], hkv_sw), rep(ones, hkv_sw)])[:, None, :]
    flag = np.concatenate([np.ones(2 * nh_na), np.zeros(nh_na), np.ones(hq_sw + hkv_sw),
                           np.zeros(hkv_sw)]).astype(np.float32)
    head_flag = jnp.asarray(np.broadcast_to(flag[:, None, None], (flag.shape[0], 1, HEAD_DIM)))

    w_in_b = w_in[0].astype(BF16)
    w_out_b = w_out[0].astype(BF16)
    w_up_b = w_up[0].astype(BF16)
    w_down_b = w_down[0].astype(BF16)
    g_attn = norm_attn[0].astype(F32)[None]
    g_mlp = norm_mlp[0].astype(F32)[None]
    sink = swa_sink[0].astype(F32)

    mproj = _inproj(meta_tokens.astype(F32), g_attn, w_in_b, head_gain, head_flag, N_META, 512)
    k0 = 3 * nh_na + hq_sw
    meta_kv = (mproj[nh_na:2 * nh_na], mproj[2 * nh_na:3 * nh_na],
               mproj[k0:k0 + hkv_sw], mproj[k0 + hkv_sw:k0 + 2 * hkv_sw])

    specs, plans = _na_plan()
    tabs = (_na_tiles(na_rpb[0], specs), plans) + _swa_tables(t5_bias)
    params = (g_attn, w_in_b, head_gain, head_flag, sink, w_out_b, g_mlp, w_up_b, w_down_b)
    y_prompt = _trunk(x_prompt, meta_kv, tabs, params)
    y_sample = _trunk(x_sample, meta_kv, tabs, params)
    return (y_prompt, y_sample)
```

```python
import functools
import math

import jax
import jax.numpy as jnp
import numpy as np
from jax import lax
from jax.experimental import pallas as pl
from jax.experimental.pallas import tpu as pltpu

F32 = jnp.float32
BF16 = jnp.bfloat16

HEAD_DIM = 128
N_META = 16
GRID_W = 64
NA_ROWS = 8
NA_COLS = 16
SWA_WINDOW = 128
SWA_BLOCK = 128
T5_BUCKETS = 32
T5_MAX_DIST = 128
NORM_EPS = 1e-6
NEG_INF = -1e30

LANES = 128
NA_QROWS = 4
NA_QBLK = NA_QROWS * GRID_W
NA_KBLKS = 3
NA_HEADS_PER_STEP = 4
NA_TILES_PER_ROW = NA_KBLKS * NA_QBLK // LANES
MIB = 1024 * 1024

assert 2 * GRID_W == LANES


def _cparams(sem, vmem_mib):
    return pltpu.CompilerParams(dimension_semantics=sem, vmem_limit_bytes=vmem_mib * MIB)


def _rms_rows(x, gain):
    ms = jnp.mean(x * x, axis=-1, keepdims=True)
    return x * lax.rsqrt(ms + NORM_EPS) * gain


def _inproj_kernel(x_ref, g_ref, w_ref, hg_ref, hf_ref, o_ref, xn_ref):
    @pl.when(pl.program_id(1) == 0)
    def _():
        xn_ref[...] = _rms_rows(x_ref[...], g_ref[...]).astype(BF16)

    res = jnp.dot(xn_ref[...], w_ref[...], preferred_element_type=F32)
    for c in range(o_ref.shape[0]):
        blk = res[:, c * LANES:(c + 1) * LANES]
        r = lax.rsqrt(jnp.mean(blk * blk, axis=-1, keepdims=True) + NORM_EPS)
        f = hf_ref[c]
        o_ref[c] = (blk * (f * r + (1.0 - f)) * hg_ref[c]).astype(BF16)


def _inproj(x2, gain, w, head_gain, head_flag, bm, bn):
    m, d = x2.shape
    n = w.shape[1]
    cpb = bn // LANES
    return pl.pallas_call(
        _inproj_kernel,
        grid=(m // bm, n // bn),
        in_specs=[
            pl.BlockSpec((bm, d), lambda i, j: (i, 0)),
            pl.BlockSpec((1, d), lambda i, j: (0, 0)),
            pl.BlockSpec((d, bn), lambda i, j: (0, j)),
            pl.BlockSpec((cpb, 1, LANES), lambda i, j: (j, 0, 0)),
            pl.BlockSpec((cpb, 1, LANES), lambda i, j: (j, 0, 0)),
        ],
        out_specs=pl.BlockSpec((cpb, bm, LANES), lambda i, j: (j, i, 0)),
        out_shape=jax.ShapeDtypeStruct((n // LANES, m, LANES), BF16),
        scratch_shapes=[pltpu.VMEM((bm, d), BF16)],
        compiler_params=_cparams(("parallel", "arbitrary"), 56),
        name="inproj",
    )(x2, gain, w, head_gain, head_flag)


def _mlp_up_kernel(x_ref, g_ref, w_ref, o_ref, xn_ref):
    @pl.when(pl.program_id(1) == 0)
    def _():
        xn_ref[...] = _rms_rows(x_ref[...], g_ref[...]).astype(BF16)

    u = jnp.dot(xn_ref[...], w_ref[...], preferred_element_type=F32)
    u = jnp.maximum(u, 0.0)
    o_ref[...] = (u * u).astype(BF16)


def _mlp_up(x2, gain, w, bm, bn):
    m, d = x2.shape
    n = w.shape[1]
    return pl.pallas_call(
        _mlp_up_kernel,
        grid=(m // bm, n // bn),
        in_specs=[
            pl.BlockSpec((bm, d), lambda i, j: (i, 0)),
            pl.BlockSpec((1, d), lambda i, j: (0, 0)),
            pl.BlockSpec((d, bn), lambda i, j: (0, j)),
        ],
        out_specs=pl.BlockSpec((bm, bn), lambda i, j: (i, j)),
        out_shape=jax.ShapeDtypeStruct((m, n), BF16),
        scratch_shapes=[pltpu.VMEM((bm, d), BF16)],
        compiler_params=_cparams(("parallel", "arbitrary"), 56),
        name="mlp_up",
    )(x2, gain, w)


def _mlp_down_kernel(a_ref, w_ref, r_ref, o_ref, acc_ref):
    k = pl.program_id(2)

    @pl.when(k == 0)
    def _():
        acc_ref[...] = jnp.zeros_like(acc_ref)

    acc_ref[...] += jnp.dot(a_ref[...], w_ref[...], preferred_element_type=F32)

    @pl.when(k == pl.num_programs(2) - 1)
    def _():
        o_ref[...] = r_ref[...] + acc_ref[...]


def _mlp_down(a, w, resid, bm, bn, bk):
    m, kd = a.shape
    n = w.shape[1]
    return pl.pallas_call(
        _mlp_down_kernel,
        grid=(m // bm, n // bn, kd // bk),
        in_specs=[
            pl.BlockSpec((bm, bk), lambda i, j, k: (i, k)),
            pl.BlockSpec((bk, bn), lambda i, j, k: (k, j)),
            pl.BlockSpec((bm, bn), lambda i, j, k: (i, j)),
        ],
        out_specs=pl.BlockSpec((bm, bn), lambda i, j, k: (i, j)),
        out_shape=jax.ShapeDtypeStruct((m, n), F32),
        scratch_shapes=[pltpu.VMEM((bm, bn), F32)],
        compiler_params=_cparams(("parallel", "parallel", "arbitrary"), 48),
        name="mlp_down",
    )(a, w, resid)


def _outproj_kernel(na_ref, sw_ref, wt_ref, wb_ref, r_ref, o_ref):
    acc = jnp.dot(na_ref[...], wt_ref[...], preferred_element_type=F32)
    acc = acc + jnp.dot(sw_ref[...], wb_ref[...], preferred_element_type=F32)
    o_ref[...] = r_ref[...] + acc


def _outproj(na_o, sw_o, w, resid, bm, bn):
    m, kh = na_o.shape
    n = w.shape[1]
    return pl.pallas_call(
        _outproj_kernel,
        grid=(n // bn, m // bm),
        in_specs=[
            pl.BlockSpec((bm, kh), lambda j, i: (i, 0)),
            pl.BlockSpec((bm, kh), lambda j, i: (i, 0)),
            pl.BlockSpec((kh, bn), lambda j, i: (0, j), pipeline_mode=pl.Buffered(1)),
            pl.BlockSpec((kh, bn), lambda j, i: (1, j), pipeline_mode=pl.Buffered(1)),
            pl.BlockSpec((bm, bn), lambda j, i: (i, j)),
        ],
        out_specs=pl.BlockSpec((bm, bn), lambda j, i: (i, j)),
        out_shape=jax.ShapeDtypeStruct((m, n), F32),
        compiler_params=_cparams(("parallel", "parallel"), 56),
        name="outproj",
    )(na_o, sw_o, w, w, resid)


def _qkt(q, k):
    return lax.dot_general(q, k, (((1,), (1,)), ((), ())), preferred_element_type=F32)


def _na_plan():
    vblk = 6
    vrows = vblk * NA_QROWS
    specs, plans = [], []
    for j in (0, 2, vblk - 1):
        ws = int(np.clip(j - 1, 0, vblk - NA_KBLKS))
        plan = []
        for qrl in range(NA_QROWS):
            qr = NA_QROWS * j + qrl
            rs = int(np.clip(qr - NA_ROWS // 2, 0, vrows - NA_ROWS))
            row = []
            for t in range(NA_TILES_PER_ROW):
                kr0 = NA_QROWS * ws + 2 * t
                halves = tuple(kr - qr if rs <= kr < rs + NA_ROWS else None for kr in (kr0, kr0 + 1))
                if halves == (None, None):
                    row.append(-1)
                else:
                    if halves not in specs:
                        specs.append(halves)
                    row.append(specs.index(halves))
            plan.append(row)
        plans.append(plan)
    return specs, plans


def _na_tiles(rpb, specs):
    nh = rpb.shape[0]
    c = np.arange(GRID_W)
    cs = np.clip(c - NA_COLS // 2, 0, GRID_W - NA_COLS)
    col_in = (c[None, :] >= cs[:, None]) & (c[None, :] < cs[:, None] + NA_COLS)
    dc = np.clip(c[None, :] - c[:, None], -(NA_COLS - 1), NA_COLS - 1) + (NA_COLS - 1)
    onehot = np.zeros((2 * NA_COLS - 1, GRID_W * GRID_W), np.float32)
    onehot[dc.reshape(-1), np.arange(GRID_W * GRID_W)] = 1.0
    tcol = jnp.einsum("hab,bx->hax", rpb.astype(F32), onehot, precision=lax.Precision.HIGHEST)
    tcol = jnp.where(col_in[None, None], tcol.reshape(nh, 2 * NA_ROWS - 1, GRID_W, GRID_W), NEG_INF)
    masked = jnp.full((nh, GRID_W, GRID_W), NEG_INF, F32)

    def half(dr):
        return masked if dr is None else tcol[:, dr + NA_ROWS - 1]

    return jnp.stack([jnp.concatenate([half(l), half(r)], axis=-1) for l, r in specs], axis=1)


def _na_kernel(q_ref, k0_ref, k1_ref, k2_ref, v0_ref, v1_ref, v2_ref, km_ref, vm_ref, tiles_ref,
               o_ref, bias_ref, *, scale, plans):
    hb = q_ref.shape[0]
    j = pl.program_id(2)
    last = pl.num_programs(2) - 1

    def build(plan):
        def body():
            for hh in range(hb):
                for qrl in range(NA_QROWS):
                    for t in range(NA_TILES_PER_ROW):
                        idx = plan[qrl][t]
                        tile = (jnp.full((GRID_W, LANES), NEG_INF, F32) if idx < 0
                                else tiles_ref[hh, idx])
                        bias_ref[hh, qrl * GRID_W:(qrl + 1) * GRID_W, t * LANES:(t + 1) * LANES] = tile
        return body

    pl.when(j == 0)(build(plans[0]))
    pl.when(j == 1)(build(plans[1]))
    pl.when(j == last)(build(plans[2]))

    for hh in range(hb):
        q = q_ref[hh]
        k = jnp.concatenate([k0_ref[hh], k1_ref[hh], k2_ref[hh]], axis=0)
        v = jnp.concatenate([v0_ref[hh], v1_ref[hh], v2_ref[hh]], axis=0)
        s = _qkt(q, k) * scale + bias_ref[hh]
        sm = _qkt(q, km_ref[hh]) * scale
        mx = jnp.maximum(jnp.max(s, axis=-1, keepdims=True), jnp.max(sm, axis=-1, keepdims=True))
        p = jnp.exp(s - mx)
        pm = jnp.exp(sm - mx)
        den = jnp.sum(p, axis=-1, keepdims=True) + jnp.sum(pm, axis=-1, keepdims=True)
        o = jnp.dot(p.astype(BF16), v, preferred_element_type=F32)
        o = o + jnp.dot(pm.astype(BF16), vm_ref[hh], preferred_element_type=F32)
        o_ref[:, hh * LANES:(hh + 1) * LANES] = (o / den).astype(BF16)


def _na_attention(proj, km, vm, tiles, plans, batch, rows):
    nh = km.shape[0]
    hb = NA_HEADS_PER_STEP
    nblk = rows // NA_QROWS
    tokens = proj.shape[1]
    assert nblk >= NA_KBLKS and nh % hb == 0
    hblks = nh // hb

    def kv_map(base, d):
        def f(h, b, j):
            ws = jnp.clip(j - 1, 0, nblk - NA_KBLKS)
            return (base + h, b * nblk + ws + d, 0)
        return f

    qspec = pl.BlockSpec((hb, NA_QBLK, LANES), lambda h, b, j: (h, b * nblk + j, 0))
    kspecs = [pl.BlockSpec((hb, NA_QBLK, LANES), kv_map(hblks, d)) for d in range(NA_KBLKS)]
    vspecs = [pl.BlockSpec((hb, NA_QBLK, LANES), kv_map(2 * hblks, d)) for d in range(NA_KBLKS)]
    mspec = pl.BlockSpec((hb, N_META, LANES), lambda h, b, j: (h, 0, 0))
    tspec = pl.BlockSpec((hb,) + tiles.shape[1:], lambda h, b, j: (h, 0, 0, 0))
    return pl.pallas_call(
        functools.partial(_na_kernel, scale=HEAD_DIM ** -0.5, plans=plans),
        grid=(hblks, batch, nblk),
        in_specs=[qspec] + kspecs + vspecs + [mspec, mspec, tspec],
        out_specs=pl.BlockSpec((NA_QBLK, hb * LANES), lambda h, b, j: (b * nblk + j, h)),
        out_shape=jax.ShapeDtypeStruct((tokens, nh * LANES), BF16),
        scratch_shapes=[pltpu.VMEM((hb, NA_QBLK, NA_KBLKS * NA_QBLK), F32)],
        compiler_params=_cparams(("arbitrary", "arbitrary", "arbitrary"), 32),
        name="na_attn",
    )(proj, proj, proj, proj, proj, proj, proj, km, vm, tiles)


def _swa_kernel(sink_ref, q_ref, k0_ref, k1_ref, k2_ref, v0_ref, v1_ref, v2_ref, km_ref, vm_ref,
                bw_ref, bm_ref, o_ref, *, scale, group):
    hk = pl.program_id(0)
    blk = q_ref.shape[1]
    q = q_ref[...].reshape(group * blk, LANES)
    k = jnp.concatenate([k0_ref[...], k1_ref[...], k2_ref[...]], axis=0)
    v = jnp.concatenate([v0_ref[...], v1_ref[...], v2_ref[...]], axis=0)
    s = _qkt(q, k) * scale + bw_ref[...].reshape(group * blk, 3 * blk)
    sm =_qkt(q, km_ref[...]) * scale + bm_ref[...].reshape(group * blk, N_META)
    row = lax.broadcasted_iota(jnp.int32, (group * blk, 1), 0)
    sink = jnp.zeros((group * blk, 1), F32)
    for g in range(group):
        sink = jnp.where(row // blk == g, sink_ref[hk * group + g], sink)
    mx = jnp.maximum(jnp.max(s, axis=-1, keepdims=True), jnp.max(sm, axis=-1, keepdims=True))
    mx = jnp.maximum(mx, sink)
    p = jnp.exp(s - mx)
    pm = jnp.exp(sm - mx)
    den = (jnp.sum(p, axis=-1, keepdims=True) + jnp.sum(pm, axis=-1, keepdims=True)
           + jnp.exp(sink - mx))
    o = jnp.dot(p.astype(BF16), v, preferred_element_type=F32)
    o = o + jnp.dot(pm.astype(BF16), vm_ref[...], preferred_element_type=F32)
    o = (o / den).astype(BF16)
    for g in range(group):
        o_ref[:, g * LANES:(g + 1) * LANES] = o[g * blk:(g + 1) * blk]


def _swa_attention(proj, km, vm, bias_w, bias_m, sink, batch, seq, q_base, k_base, v_base):
    hkv = km.shape[0]
    hq = bias_w.shape[1]
    group = hq // hkv
    blk = SWA_BLOCK
    nb = seq // blk
    tokens = proj.shape[1]

    def bw_map(hk, b, i):
        variant = jnp.where(i == 0, 1, 0) + jnp.where(i == nb - 1, 2, 0)
        return (variant, hk, 0, 0)

    def kv_map(base, d):
        def f(hk, b, i):
            return (base + hk, b * nb + jnp.clip(i - 1 + d, 0, nb - 1), 0)
        return f

    kspecs = [pl.BlockSpec((None, blk, LANES), kv_map(k_base, d)) for d in range(3)]
    vspecs = [pl.BlockSpec((None, blk, LANES), kv_map(v_base, d)) for d in range(3)]
    mspec = pl.BlockSpec((None, N_META, LANES), lambda hk, b, i: (hk, 0, 0))
    return pl.pallas_call(
        functools.partial(_swa_kernel, scale=HEAD_DIM ** -0.5, group=group),
        grid=(hkv, batch, nb),
        in_specs=[pl.BlockSpec(memory_space=pltpu.SMEM),
                  pl.BlockSpec((group, blk, LANES),
                               lambda hk, b, i: (q_base // group + hk, b * nb + i, 0))]
                 + kspecs + vspecs + [mspec, mspec,
                  pl.BlockSpec((None, group, blk, 3 * blk), bw_map),
                  pl.BlockSpec((None, group, blk, N_META),
                               lambda hk, b, i: (jnp.minimum(i, 1), hk, 0, 0))],
        out_specs=pl.BlockSpec((blk, group * LANES), lambda hk, b, i: (b * nb + i, hk)),
        out_shape=jax.ShapeDtypeStruct((tokens, hq * LANES), BF16),
        compiler_params=_cparams(("parallel", "parallel", "arbitrary"), 32),
        name="swa_attn",
    )(sink, proj, proj, proj, proj, proj, proj, proj, km, vm, bias_w, bias_m)


def _t5_bucket(rel):
    nb = T5_BUCKETS // 2
    max_exact = nb // 2
    ret = np.where(rel > 0, nb, 0)
    n = np.abs(rel)
    large = max_exact + (np.log(np.maximum(n, 1) / max_exact)
                         / math.log(T5_MAX_DIST / max_exact) * (nb - max_exact)).astype(np.int64)
    large = np.minimum(large, nb - 1)
    return ret + np.where(n < max_exact, n, large)


def _select_rows(table, idx):
    flat = idx.reshape(-1)
    onehot = np.zeros((table.shape[0], flat.size), np.float32)
    onehot[flat, np.arange(flat.size)] = 1.0
    out = jnp.einsum("bh,bx->xh", table.astype(F32), onehot, precision=lax.Precision.HIGHEST)
    return out.reshape(idx.shape + (table.shape[1],))


def _swa_tables(t5_bias):
    blk = SWA_BLOCK
    rel = np.arange(-(2 * blk - 1), 2 * blk)
    by_rel = _select_rows(t5_bias, _t5_bucket(rel)).T
    by_rel = jnp.where((np.abs(rel) <= SWA_WINDOW)[None], by_rel, NEG_INF)
    bias_w = jnp.stack([by_rel[:, blk - 1 - q:blk - 1 - q + 3 * blk] for q in range(blk)], axis=1)
    jj = np.arange(3 * blk)
    edge = np.stack([np.zeros_like(jj, bool), jj < blk, jj >= 2 * blk, (jj < blk) | (jj >= 2 * blk)])
    bias_w = jnp.where(edge[:, None, None, :], NEG_INF, bias_w[None])
    assert N_META + blk - (N_META - 1) > T5_MAX_DIST
    rel_m = np.arange(N_META)[None, :] - (N_META + np.arange(2 * blk))[:, None]
    bias_m = _select_rows(t5_bias, _t5_bucket(rel_m))
    bias_m = bias_m.reshape(2, blk, N_META, -1).transpose(0, 3, 1, 2)
    return bias_w, bias_m


def _trunk(x, meta_kv, tabs, params):
    batch, seq, d = x.shape
    (g_attn, w_in, head_gain, head_flag, sink, w_out, g_mlp, w_up, w_down) = params
    km_na, vm_na, km_sw, vm_sw = meta_kv
    na_tiles, na_plans, bias_w, bias_m = tabs
    nh_na = km_na.shape[0]
    hq_sw = bias_w.shape[1]
    hkv_sw = km_sw.shape[0]
    x2 = x.reshape(batch * seq, d)
    proj = _inproj(x2, g_attn, w_in, head_gain, head_flag, 512, 1024)
    rows = seq // GRID_W
    assert seq % (GRID_W * NA_QROWS) == 0 and rows >= NA_ROWS
    assert seq % SWA_BLOCK == 0
    na_o = _na_attention(proj, km_na, vm_na, na_tiles, na_plans, batch, rows)
    q_base = 3 * nh_na
    k_base = q_base + hq_sw
    v_base = k_base + hkv_sw
    sw_o = _swa_attention(proj, km_sw, vm_sw, bias_w, bias_m, sink, batch, seq,
                          q_base, k_base, v_base)
    x1 = _outproj(na_o, sw_o, w_out, x2, 512, 2048)
    u = _mlp_up(x1, g_mlp, w_up, 512, 1024)
    y = _mlp_down(u, w_down, x1, 1024, 1024, 2048)
    return y.reshape(batch, seq, d)


def kernel(x_prompt, x_sample, meta_tokens, t5_bias, norm_attn, w_in, q_norm_na, k_norm_na, na_rpb,
           q_norm_swa, k_norm_swa, swa_sink, w_out, norm_mlp, w_up, w_down):
    depth = w_in.shape[0]
    assert depth == 1
    nh_na = na_rpb.shape[1]
    hq_sw = swa_sink.shape[1]
    in_width = w_in.shape[2]
    hkv_sw = (in_width // HEAD_DIM - 3 * nh_na - hq_sw) // 2

    ones = jnp.ones((HEAD_DIM,), F32)

    def rep(g, n):
        return jnp.broadcast_to(g.astype(F32)[None], (n, HEAD_DIM))

    head_gain = jnp.concatenate([
        rep(q_norm_na[0], nh_na), rep(k_norm_na[0], nh_na), rep(ones, nh_na),
        rep(q_norm_swa[0], hq_sw), rep(k_norm_swa[0], hkv_sw), rep(ones, hkv_sw)])[:, None, :]
    flag = np.concatenate([np.ones(2 * nh_na), np.zeros(nh_na), np.ones(hq_sw + hkv_sw),
                           np.zeros(hkv_sw)]).astype(np.float32)
    head_flag = jnp.asarray(np.broadcast_to(flag[:, None, None], (flag.shape[0], 1, HEAD_DIM)))

    w_in_b = w_in[0].astype(BF16)
    w_out_b = w_out[0].astype(BF16)
    w_up_b = w_up[0].astype(BF16)
    w_down_b = w_down[0].astype(BF16)
    g_attn = norm_attn[0].astype(F32)[None]
    g_mlp = norm_mlp[0].astype(F32)[None]
    sink = swa_sink[0].astype(F32)

    mproj = _inproj(meta_tokens.astype(F32), g_attn, w_in_b, head_gain, head_flag, N_META, 512)
    k0 = 3 * nh_na + hq_sw
    meta_kv = (mproj[nh_na:2 * nh_na], mproj[2 * nh_na:3 * nh_na],
               mproj[k0:k0 + hkv_sw], mproj[k0 + hkv_sw:k0 + 2 * hkv_sw])

    specs, plans = _na_plan()
    tabs = (_na_tiles(na_rpb[0], specs), plans) + _swa_tables(t5_bias)
    params = (g_attn, w_in_b, head_gain, head_flag, sink, w_out_b, g_mlp, w_up_b, w_down_b)
    y_prompt = _trunk(x_prompt, meta_kv, tabs, params)
    y_sample = _trunk(x_sample, meta_kv, tabs, params)
    return (y_prompt, y_sample)
```

```python
import functools
import math

import jax
import jax.numpy as jnp
import numpy as np
from jax import lax
from jax.experimental import pallas as pl
from jax.experimental.pallas import tpu as pltpu

F32 = jnp.float32
BF16 = jnp.bfloat16

HEAD_DIM = 128
N_META = 16
GRID_W = 64
NA_ROWS = 8
NA_COLS = 16
SWA_WINDOW = 128
SWA_BLOCK = 128
T5_BUCKETS = 32
T5_MAX_DIST = 128
NORM_EPS = 1e-6
NEG_INF = -1e30
LOG2_E = math.log2(math.e)

LANES = 128
NA_QROWS = 4
NA_QBLK = NA_QROWS * GRID_W
NA_KBLKS = 3
NA_HEADS_PER_STEP = 8
SWA_KV_HEADS_PER_STEP = 4
NA_TILES_PER_ROW = NA_KBLKS * NA_QBLK // LANES
MIB = 1024 * 1024

assert 2 * GRID_W == LANES


def _cparams(sem, vmem_mib):
    return pltpu.CompilerParams(dimension_semantics=sem, vmem_limit_bytes=vmem_mib * MIB)


def _rms_rows(x, gain):
    ms = jnp.mean(x * x, axis=-1, keepdims=True)
    return x * lax.rsqrt(ms + NORM_EPS) * gain


def _inproj_kernel(x_ref, g_ref, w_ref, hg_ref, hf_ref, o_ref, xn_ref):
    @pl.when(pl.program_id(1) == 0)
    def _():
        xn_ref[...] = _rms_rows(x_ref[...], g_ref[...]).astype(BF16)

    res = jnp.dot(xn_ref[...], w_ref[...], preferred_element_type=F32)
    for c in range(o_ref.shape[0]):
        blk = res[:, c * LANES:(c + 1) * LANES]
        r = lax.rsqrt(jnp.mean(blk * blk, axis=-1, keepdims=True) + NORM_EPS)
        f = hf_ref[c]
        o_ref[c] = (blk * (f * r + (1.0 - f)) * hg_ref[c]).astype(BF16)


def _inproj(x2, gain, w, head_gain, head_flag, bm, bn):
    m, d = x2.shape
    n = w.shape[1]
    cpb = bn // LANES
    return pl.pallas_call(
        _inproj_kernel,
        grid=(m // bm, n // bn),
        in_specs=[
            pl.BlockSpec((bm, d), lambda i, j: (i, 0)),
            pl.BlockSpec((1, d), lambda i, j: (0, 0)),
            pl.BlockSpec((d, bn), lambda i, j: (0, j)),
            pl.BlockSpec((cpb, 1, LANES), lambda i, j: (j, 0, 0)),
            pl.BlockSpec((cpb, 1, LANES), lambda i, j: (j, 0, 0)),
        ],
        out_specs=pl.BlockSpec((cpb, bm, LANES), lambda i, j: (j, i, 0)),
        out_shape=jax.ShapeDtypeStruct((n // LANES, m, LANES), BF16),
        scratch_shapes=[pltpu.VMEM((bm, d), BF16)],
        compiler_params=_cparams(("parallel", "arbitrary"), 56),
        name="inproj",
    )(x2, gain, w, head_gain, head_flag)


def _mlp_up_kernel(x_ref, g_ref, w_ref, o_ref, xn_ref):
    @pl.when(pl.program_id(1) == 0)
    def _():
        xn_ref[...] = _rms_rows(x_ref[...], g_ref[...]).astype(BF16)

    u = jnp.dot(xn_ref[...], w_ref[...], preferred_element_type=F32)
    u = jnp.maximum(u, 0.0)
    o_ref[...] = (u * u).astype(BF16)


def _mlp_up(x2, gain, w, bm, bn):
    m, d = x2.shape
    n = w.shape[1]
    return pl.pallas_call(
        _mlp_up_kernel,
        grid=(m // bm, n // bn),
        in_specs=[
            pl.BlockSpec((bm, d), lambda i, j: (i, 0)),
            pl.BlockSpec((1, d), lambda i, j: (0, 0)),
            pl.BlockSpec((d, bn), lambda i, j: (0, j)),
        ],
        out_specs=pl.BlockSpec((bm, bn), lambda i, j: (i, j)),
        out_shape=jax.ShapeDtypeStruct((m, n), BF16),
        scratch_shapes=[pltpu.VMEM((bm, d), BF16)],
        compiler_params=_cparams(("parallel", "arbitrary"), 56),
        name="mlp_up",
    )(x2, gain, w)


def _mlp_down_kernel(a_ref, w_ref, r_ref, o_ref, acc_ref):
    k = pl.program_id(2)

    @pl.when(k == 0)
    def _():
        acc_ref[...] = jnp.zeros_like(acc_ref)

    acc_ref[...] += jnp.dot(a_ref[...], w_ref[...], preferred_element_type=F32)

    @pl.when(k == pl.num_programs(2) - 1)
    def _():
        o_ref[...] = r_ref[...] + acc_ref[...]


def _mlp_down(a, w, resid, bm, bn, bk):
    m, kd = a.shape
    n = w.shape[1]
    return pl.pallas_call(
        _mlp_down_kernel,
        grid=(m // bm, n // bn, kd // bk),
        in_specs=[
            pl.BlockSpec((bm, bk), lambda i, j, k: (i, k)),
            pl.BlockSpec((bk, bn), lambda i, j, k: (k, j)),
            pl.BlockSpec((bm, bn), lambda i, j, k: (i, j)),
        ],
        out_specs=pl.BlockSpec((bm, bn), lambda i, j, k: (i, j)),
        out_shape=jax.ShapeDtypeStruct((m, n), F32),
        scratch_shapes=[pltpu.VMEM((bm, bn), F32)],
        compiler_params=_cparams(("parallel", "parallel", "arbitrary"), 48),
        name="mlp_down",
    )(a, w, resid)


def _outproj_kernel(na_ref, sw_ref, wt_ref, wb_ref, r_ref, o_ref):
    acc = jnp.dot(na_ref[...], wt_ref[...], preferred_element_type=F32)
    acc = acc + jnp.dot(sw_ref[...], wb_ref[...], preferred_element_type=F32)
    o_ref[...] = r_ref[...] + acc


def _outproj(na_o, sw_o, w, resid, bm, bn):
    m, kh = na_o.shape
    n = w.shape[1]
    return pl.pallas_call(
        _outproj_kernel,
        grid=(n // bn, m // bm),
        in_specs=[
            pl.BlockSpec((bm, kh), lambda j, i: (i, 0)),
            pl.BlockSpec((bm, kh), lambda j, i: (i, 0)),
            pl.BlockSpec((kh, bn), lambda j, i: (0, j), pipeline_mode=pl.Buffered(1)),
            pl.BlockSpec((kh, bn), lambda j, i: (1, j), pipeline_mode=pl.Buffered(1)),
            pl.BlockSpec((bm, bn), lambda j, i: (i, j)),
        ],
        out_specs=pl.BlockSpec((bm, bn), lambda j, i: (i, j)),
        out_shape=jax.ShapeDtypeStruct((m, n), F32),
        compiler_params=_cparams(("parallel", "parallel"), 56),
        name="outproj",
    )(na_o, sw_o, w, w, resid)


def _qkt(q, k):
    return lax.dot_general(q, k, (((1,), (1,)), ((), ())), preferred_element_type=F32)


def _na_plan():
    vblk = 6
    vrows = vblk * NA_QROWS
    specs, plans = [], []
    for j in (0, 2, vblk - 1):
        ws = int(np.clip(j - 1, 0, vblk - NA_KBLKS))
        plan = []
        for qrl in range(NA_QROWS):
            qr = NA_QROWS * j + qrl
            rs = int(np.clip(qr - NA_ROWS // 2, 0, vrows - NA_ROWS))
            row = []
            for t in range(NA_TILES_PER_ROW):
                kr0 = NA_QROWS * ws + 2 * t
                halves = tuple(kr - qr if rs <= kr < rs + NA_ROWS else None for kr in (kr0, kr0 + 1))
                if halves == (None, None):
                    row.append(-1)
                else:
                    if halves not in specs:
                        specs.append(halves)
                    row.append(specs.index(halves))
            plan.append(row)
        plans.append(plan)
    return specs, plans


def _na_tiles(rpb, specs):
    nh = rpb.shape[0]
    c = np.arange(GRID_W)
    cs = np.clip(c - NA_COLS // 2, 0, GRID_W - NA_COLS)
    col_in = (c[None, :] >= cs[:, None]) & (c[None, :] < cs[:, None] + NA_COLS)
    dc = np.clip(c[None, :] - c[:, None], -(NA_COLS - 1), NA_COLS - 1) + (NA_COLS - 1)
    onehot = np.zeros((2 * NA_COLS - 1, GRID_W * GRID_W), np.float32)
    onehot[dc.reshape(-1), np.arange(GRID_W * GRID_W)] = 1.0
    tcol = jnp.einsum("hab,bx->hax", rpb.astype(F32), onehot, precision=lax.Precision.HIGHEST)
    tcol = jnp.where(col_in[None, None], tcol.reshape(nh, 2 * NA_ROWS - 1, GRID_W, GRID_W), NEG_INF)
    masked = jnp.full((nh, GRID_W, GRID_W), NEG_INF, F32)

    def half(dr):
        return masked if dr is None else tcol[:, dr + NA_ROWS - 1]

    tiles = jnp.stack([jnp.concatenate([half(l), half(r)], axis=-1) for l, r in specs], axis=1)
    return tiles * LOG2_E


def _na_kernel(q_ref, k0_ref, k1_ref, k2_ref, v0_ref, v1_ref, v2_ref, km_ref, vm_ref, tiles_ref,
               o_ref, bias_ref, *, scale, plans):
    hb = q_ref.shape[0]
    j = pl.program_id(2)
    last = pl.num_programs(2) - 1
    lane = lax.broadcasted_iota(jnp.int32, (GRID_W, LANES), 1)
    meta_tile = jnp.where(lane < N_META, 0.0, NEG_INF).astype(F32)
    masked_tile = jnp.full((GRID_W, LANES), NEG_INF, F32)

    def build(plan):
        def body():
            for hh in range(hb):
                for qrl in range(NA_QROWS):
                    for t in range(NA_TILES_PER_ROW + 1):
                        if t == NA_TILES_PER_ROW:
                            tile = meta_tile
                        else:
                            idx = plan[qrl][t]
                            tile = masked_tile if idx < 0 else tiles_ref[hh, idx]
                        bias_ref[hh, qrl * GRID_W:(qrl + 1) * GRID_W, t * LANES:(t + 1) * LANES] = tile
        return body

    pl.when(j == 0)(build(plans[0]))
    pl.when(j == 1)(build(plans[1]))
    pl.when(j == last)(build(plans[2]))

    for hh in range(hb):
        q = q_ref[hh]
        k = jnp.concatenate([k0_ref[hh], k1_ref[hh], k2_ref[hh], km_ref[hh]], axis=0)
        v = jnp.concatenate([v0_ref[hh], v1_ref[hh], v2_ref[hh], vm_ref[hh]], axis=0)
        s = _qkt(q, k) * (scale * LOG2_E) + bias_ref[hh]
        p = jnp.exp2(s - jnp.max(s, axis=-1, keepdims=True))
        den = jnp.sum(p, axis=-1, keepdims=True)
        o = jnp.dot(p.astype(BF16), v, preferred_element_type=F32)
        o_ref[:, hh * LANES:(hh + 1) * LANES] = (o / den).astype(BF16)


def _na_attention(proj, km, vm, tiles, plans, batch, rows):
    nh = km.shape[0]
    hb = NA_HEADS_PER_STEP
    nblk = rows // NA_QROWS
    tokens = proj.shape[1]
    assert nblk >= NA_KBLKS and nh % hb == 0
    hblks = nh // hb

    def kv_map(base, d):
        def f(h, b, j):
            ws = jnp.clip(j - 1, 0, nblk - NA_KBLKS)
            return (base + h, b * nblk + ws + d, 0)
        return f

    qspec = pl.BlockSpec((hb, NA_QBLK, LANES), lambda h, b, j: (h, b * nblk + j, 0))
    kspecs = [pl.BlockSpec((hb, NA_QBLK, LANES), kv_map(hblks, d)) for d in range(NA_KBLKS)]
    vspecs = [pl.BlockSpec((hb, NA_QBLK, LANES), kv_map(2 * hblks, d)) for d in range(NA_KBLKS)]
    mspec = pl.BlockSpec((hb, LANES, LANES), lambda h, b, j: (h, 0, 0))
    tspec = pl.BlockSpec((hb,) + tiles.shape[1:], lambda h, b, j: (h, 0, 0, 0))
    return pl.pallas_call(
        functools.partial(_na_kernel, scale=HEAD_DIM ** -0.5, plans=plans),
        grid=(hblks, batch, nblk),
        in_specs=[qspec] + kspecs + vspecs + [mspec, mspec, tspec],
        out_specs=pl.BlockSpec((NA_QBLK, hb * LANES), lambda h, b, j: (b * nblk + j, h)),
        out_shape=jax.ShapeDtypeStruct((tokens, nh * LANES), BF16),
        scratch_shapes=[pltpu.VMEM((hb, NA_QBLK, NA_KBLKS * NA_QBLK + LANES), F32)],
        compiler_params=_cparams(("arbitrary", "arbitrary", "arbitrary"), 48),
        name="na_attn",
    )(proj, proj, proj, proj, proj, proj, proj, km, vm, tiles)


def _swa_kernel(q_ref, k0_ref, k1_ref, k2_ref, v0_ref, v1_ref, v2_ref, km_ref, vm_ref, bias_ref,
                o_ref, *, scale, group):
    hkb = k0_ref.shape[0]
    blk = q_ref.shape[1]
    for kh in range(hkb):
        heads = slice(kh * group, (kh + 1) * group)
        q = q_ref[heads].reshape(group * blk, LANES)
        k = jnp.concatenate([k0_ref[kh], k1_ref[kh], k2_ref[kh], km_ref[kh]], axis=0)
        v = jnp.concatenate([v0_ref[kh], v1_ref[kh], v2_ref[kh], vm_ref[kh]], axis=0)
        s = _qkt(q, k) * (scale * LOG2_E) + bias_ref[heads].reshape(group * blk, 3 * blk + LANES)
        p = jnp.exp2(s - jnp.max(s, axis=-1, keepdims=True))
        den = jnp.sum(p, axis=-1, keepdims=True)
        o = (jnp.dot(p.astype(BF16), v, preferred_element_type=F32) / den).astype(BF16)
        for g in range(group):
            col = (kh * group + g) * LANES
            o_ref[:, col:col + LANES] = o[g * blk:(g + 1) * blk]


def _swa_attention(proj, km, vm, bias, batch, seq, q_base, k_base, v_base):
    hkv = km.shape[0]
    hq = bias.shape[1]
    group = hq // hkv
    hkb = SWA_KV_HEADS_PER_STEP
    blk = SWA_BLOCK
    nb = seq // blk
    tokens = proj.shape[1]
    assert hkv % hkb == 0 and k_base % hkb == 0 and v_base % hkb == 0 and q_base % (hkb * group) == 0

    def bias_map(hk, b, i):
        variant = jnp.where(i == 0, 1, 0) + jnp.where(i == nb - 1, 2, 0)
        return (variant, hk, 0, 0)

    def kv_map(base, d):
        def f(hk, b, i):
            return (base // hkb + hk, b * nb + jnp.clip(i - 1 + d, 0, nb - 1), 0)
        return f

    kspecs = [pl.BlockSpec((hkb, blk, LANES), kv_map(k_base, d)) for d in range(3)]
    vspecs = [pl.BlockSpec((hkb, blk, LANES), kv_map(v_base, d)) for d in range(3)]
    mspec = pl.BlockSpec((hkb, LANES, LANES), lambda hk, b, i: (hk, 0, 0))
    return pl.pallas_call(
        functools.partial(_swa_kernel, scale=HEAD_DIM ** -0.5, group=group),
        grid=(hkv // hkb, batch, nb),
        in_specs=[pl.BlockSpec((hkb * group, blk, LANES),
                               lambda hk, b, i: (q_base // (hkb * group) + hk, b * nb + i, 0))]
                 + kspecs + vspecs + [mspec, mspec,
                  pl.BlockSpec((None, hkb * group, blk, 3 * blk + LANES), bias_map)],
        out_specs=pl.BlockSpec((blk, hkb * group * LANES), lambda hk, b, i: (b * nb + i, hk)),
        out_shape=jax.ShapeDtypeStruct((tokens, hq * LANES), BF16),
        compiler_params=_cparams(("parallel", "parallel", "arbitrary"), 32),
        name="swa_attn",
    )(proj, proj, proj, proj, proj, proj, proj, km, vm, bias)


def _t5_bucket(rel):
    nb = T5_BUCKETS // 2
    max_exact = nb // 2
    ret = np.where(rel > 0, nb, 0)
    n = np.abs(rel)
    large = max_exact + (np.log(np.maximum(n, 1) / max_exact)
                         / math.log(T5_MAX_DIST / max_exact) * (nb - max_exact)).astype(np.int64)
    large = np.minimum(large, nb - 1)
    return ret + np.where(n < max_exact, n, large)


def _select_rows(table, idx):
    flat = idx.reshape(-1)
    onehot = np.zeros((table.shape[0], flat.size), np.float32)
    onehot[flat, np.arange(flat.size)] = 1.0
    out = jnp.einsum("bh,bx->xh", table.astype(F32), onehot, precision=lax.Precision.HIGHEST)
    return out.reshape(idx.shape + (table.shape[1],))


def _swa_tables(t5_bias, sink):
    blk = SWA_BLOCK
    rel = np.arange(-(2 * blk - 1), 2 * blk)
    by_rel = _select_rows(t5_bias, _t5_bucket(rel)).T
    by_rel = jnp.where((np.abs(rel) <= SWA_WINDOW)[None], by_rel, NEG_INF)
    bias_w = jnp.stack([by_rel[:, blk - 1 - q:blk - 1 - q + 3 * blk] for q in range(blk)], axis=1)
    jj = np.arange(3 * blk)
    edge = np.stack([np.zeros_like(jj, bool), jj < blk, jj >= 2 * blk, (jj < blk) | (jj >= 2 * blk)])
    bias_w = jnp.where(edge[:, None, None, :], NEG_INF, bias_w[None])
    assert N_META + blk - (N_META - 1) > T5_MAX_DIST
    rel_m = np.arange(N_META)[None, :] - (N_META + np.arange(2 * blk))[:, None]
    bias_m = _select_rows(t5_bias, _t5_bucket(rel_m))
    bias_m = bias_m.reshape(2, blk, N_META, -1).transpose(0, 3, 1, 2)
    bias_m = jnp.stack([bias_m[1], bias_m[0], bias_m[1], bias_m[0]])
    hq = bias_m.shape[1]
    sink_col = jnp.broadcast_to(sink.astype(F32)[None, :, None, None], (4, hq, blk, 1))
    pad = jnp.full((4, hq, blk, LANES - N_META - 1), NEG_INF, F32)
    return jnp.concatenate([bias_w, bias_m, sink_col, pad], axis=-1) * LOG2_E


def _trunk(x, meta_kv, tabs, params):
    batch, seq, d = x.shape
    (g_attn, w_in, head_gain, head_flag, w_out, g_mlp, w_up, w_down) = params
    km_na, vm_na, km_sw, vm_sw = meta_kv
    na_tiles, na_plans, swa_bias = tabs
    nh_na = km_na.shape[0]
    hq_sw = swa_bias.shape[1]
    hkv_sw = km_sw.shape[0]
    x2 = x.reshape(batch * seq, d)
    proj = _inproj(x2, g_attn, w_in, head_gain, head_flag, 512, 1024)
    rows = seq // GRID_W
    assert seq % (GRID_W * NA_QROWS) == 0 and rows >= NA_ROWS
    assert seq % SWA_BLOCK == 0
    na_o = _na_attention(proj, km_na, vm_na, na_tiles, na_plans, batch, rows)
    q_base = 3 * nh_na
    k_base = q_base + hq_sw
    v_base = k_base + hkv_sw
    sw_o = _swa_attention(proj, km_sw, vm_sw, swa_bias, batch, seq, q_base, k_base, v_base)
    x1 = _outproj(na_o, sw_o, w_out, x2, 512, 2048)
    u = _mlp_up(x1, g_mlp, w_up, 512, 1024)
    y = _mlp_down(u, w_down, x1, 1024, 1024, 2048)
    return y.reshape(batch, seq, d)


def kernel(x_prompt, x_sample, meta_tokens, t5_bias, norm_attn, w_in, q_norm_na, k_norm_na, na_rpb,
           q_norm_swa, k_norm_swa, swa_sink, w_out, norm_mlp, w_up, w_down):
    depth = w_in.shape[0]
    assert depth == 1
    nh_na = na_rpb.shape[1]
    hq_sw = swa_sink.shape[1]
    in_width = w_in.shape[2]
    hkv_sw = (in_width // HEAD_DIM - 3 * nh_na - hq_sw) // 2

    ones = jnp.ones((HEAD_DIM,), F32)

    def rep(g, n):
        return jnp.broadcast_to(g.astype(F32)[None], (n, HEAD_DIM))

    head_gain = jnp.concatenate([
        rep(q_norm_na[0], nh_na), rep(k_norm_na[0], nh_na), rep(ones, nh_na),
        rep(q_norm_swa[0], hq_sw), rep(k_norm_swa[0], hkv_sw), rep(ones, hkv_sw)])[:, None, :]
    flag = np.concatenate([np.ones(2 * nh_na), np.zeros(nh_na), np.ones(hq_sw + hkv_sw),
                           np.zeros(hkv_sw)]).astype(np.float32)
    head_flag = jnp.asarray(np.broadcast_to(flag[:, None, None], (flag.shape[0], 1, HEAD_DIM)))

    w_in_b = w_in[0].astype(BF16)
    w_out_b = w_out[0].astype(BF16)
    w_up_b = w_up[0].astype(BF16)
    w_down_b = w_down[0].astype(BF16)
    g_attn = norm_attn[0].astype(F32)[None]
    g_mlp = norm_mlp[0].astype(F32)[None]

    mproj = _inproj(meta_tokens.astype(F32), g_attn, w_in_b, head_gain, head_flag, N_META, 512)
    mproj = jnp.pad(mproj, ((0, 0), (0, LANES - N_META), (0, 0)))
    k0 = 3 * nh_na + hq_sw
    meta_kv = (mproj[nh_na:2 * nh_na], mproj[2 * nh_na:3 * nh_na],
               mproj[k0:k0 + hkv_sw], mproj[k0 + hkv_sw:k0 + 2 * hkv_sw])

    specs, plans = _na_plan()
    tabs = (_na_tiles(na_rpb[0], specs), plans, _swa_tables(t5_bias, swa_sink[0]))
    params = (g_attn, w_in_b, head_gain, head_flag, w_out_b, g_mlp, w_up_b, w_down_b)
    y_prompt = _trunk(x_prompt, meta_kv, tabs, params)
    y_sample = _trunk(x_sample, meta_kv, tabs, params)
    return (y_prompt, y_sample)
```

```python
import functools
import math

import jax
import jax.numpy as jnp
import numpy as np
from jax import lax
from jax.experimental import pallas as pl
from jax.experimental.pallas import tpu as pltpu

F32 = jnp.float32
BF16 = jnp.bfloat16

HEAD_DIM = 128
N_META = 16
GRID_W = 64
NA_ROWS = 8
NA_COLS = 16
SWA_WINDOW = 128
SWA_BLOCK = 128
T5_BUCKETS = 32
T5_MAX_DIST = 128
NORM_EPS = 1e-6
NEG_INF = -1e30
LOG2_E = math.log2(math.e)

LANES = 128
NA_QROWS = 4
NA_QBLK = NA_QROWS * GRID_W
NA_KBLKS = 3
NA_HEADS_PER_STEP = 8
SWA_KV_HEADS_PER_STEP = 4
NA_TILES_PER_ROW = NA_KBLKS * NA_QBLK // LANES
MIB = 1024 * 1024

assert 2 * GRID_W == LANES


def _cparams(sem, vmem_mib):
    return pltpu.CompilerParams(dimension_semantics=sem, vmem_limit_bytes=vmem_mib * MIB)


def _rms_rows(x, gain):
    ms = jnp.mean(x * x, axis=-1, keepdims=True)
    return x * lax.rsqrt(ms + NORM_EPS) * gain


def _inproj_kernel(x_ref, g_ref, w_ref, hg_ref, hf_ref, o_ref, xn_ref):
    @pl.when(pl.program_id(1) == 0)
    def _():
        xn_ref[...] = _rms_rows(x_ref[...], g_ref[...]).astype(BF16)

    res = jnp.dot(xn_ref[...], w_ref[...], preferred_element_type=F32)
    for c in range(o_ref.shape[0]):
        blk = res[:, c * LANES:(c + 1) * LANES]
        r = lax.rsqrt(jnp.mean(blk * blk, axis=-1, keepdims=True) + NORM_EPS)
        f = hf_ref[c]
        o_ref[c] = (blk * (f * r + (1.0 - f)) * hg_ref[c]).astype(BF16)


def _inproj(x2, gain, w, head_gain, head_flag, bm, bn):
    m, d = x2.shape
    n = w.shape[1]
    cpb = bn // LANES
    return pl.pallas_call(
        _inproj_kernel,
        grid=(m // bm, n // bn),
        in_specs=[
            pl.BlockSpec((bm, d), lambda i, j: (i, 0)),
            pl.BlockSpec((1, d), lambda i, j: (0, 0)),
            pl.BlockSpec((d, bn), lambda i, j: (0, j)),
            pl.BlockSpec((cpb, 1, LANES), lambda i, j: (j, 0, 0)),
            pl.BlockSpec((cpb, 1, LANES), lambda i, j: (j, 0, 0)),
        ],
        out_specs=pl.BlockSpec((cpb, bm, LANES), lambda i, j: (j, i, 0)),
        out_shape=jax.ShapeDtypeStruct((n // LANES, m, LANES), BF16),
        scratch_shapes=[pltpu.VMEM((bm, d), BF16)],
        compiler_params=_cparams(("parallel", "arbitrary"), 56),
        name="inproj",
    )(x2, gain, w, head_gain, head_flag)


def _mlp_up_kernel(a_ref, ss_ref, w_ref, o_ref, *, d):
    ss = ss_ref[0][:, :1]
    for c in range(1, ss_ref.shape[0]):
        ss = ss + ss_ref[c][:, :1]
    r = lax.rsqrt(ss * (1.0 / d) + NORM_EPS)
    u = jnp.dot(a_ref[...], w_ref[...], preferred_element_type=F32) * r
    u = jnp.maximum(u, 0.0)
    o_ref[...] = (u * u).astype(BF16)


def _mlp_up(a, ss, w, bm, bn):
    m, d = a.shape
    n = w.shape[1]
    ncol = ss.shape[0]
    return pl.pallas_call(
        functools.partial(_mlp_up_kernel, d=d),
        grid=(m // bm, n // bn),
        in_specs=[
            pl.BlockSpec((bm, d), lambda i, j: (i, 0), pipeline_mode=pl.Buffered(1)),
            pl.BlockSpec((ncol, bm, LANES), lambda i, j: (0, i, 0)),
            pl.BlockSpec((d, bn), lambda i, j: (0, j)),
        ],
        out_specs=pl.BlockSpec((bm, bn), lambda i, j: (i, j)),
        out_shape=jax.ShapeDtypeStruct((m, n), BF16),
        compiler_params=_cparams(("parallel", "arbitrary"), 56),
        name="mlp_up",
    )(a, ss, w)


def _mlp_down_kernel(a_ref, w_ref, r_ref, o_ref, acc_ref):
    k = pl.program_id(2)

    @pl.when(k == 0)
    def _():
        acc_ref[...] = jnp.zeros_like(acc_ref)

    acc_ref[...] += jnp.dot(a_ref[...], w_ref[...], preferred_element_type=F32)

    @pl.when(k == pl.num_programs(2) - 1)
    def _():
        o_ref[...] = r_ref[...] + acc_ref[...]


def _mlp_down(a, w, resid, bm, bn, bk):
    m, kd = a.shape
    n = w.shape[1]
    return pl.pallas_call(
        _mlp_down_kernel,
        grid=(m // bm, n // bn, kd // bk),
        in_specs=[
            pl.BlockSpec((bm, bk), lambda i, j, k: (i, k)),
            pl.BlockSpec((bk, bn), lambda i, j, k: (k, j)),
            pl.BlockSpec((bm, bn), lambda i, j, k: (i, j)),
        ],
        out_specs=pl.BlockSpec((bm, bn), lambda i, j, k: (i, j)),
        out_shape=jax.ShapeDtypeStruct((m, n), F32),
        scratch_shapes=[pltpu.VMEM((bm, bn), F32)],
        compiler_params=_cparams(("parallel", "parallel", "arbitrary"), 48),
        name="mlp_down",
    )(a, w, resid)


def _outproj_kernel(na_ref, sw_ref, wt_ref, wb_ref, r_ref, g_ref, o_ref, ob_ref, ss_ref):
    acc = jnp.dot(na_ref[...], wt_ref[...], preferred_element_type=F32)
    acc = acc + jnp.dot(sw_ref[...], wb_ref[...], preferred_element_type=F32)
    x1 = r_ref[...] + acc
    o_ref[...] = x1
    ob_ref[...] = (x1 * g_ref[...]).astype(BF16)
    ss_ref[...] = jnp.broadcast_to(jnp.sum(x1 * x1, axis=-1, keepdims=True), ss_ref.shape)


def _outproj(na_o, sw_o, w, resid, gain, bm, bn):
    m, kh = na_o.shape
    n = w.shape[1]
    return pl.pallas_call(
        _outproj_kernel,
        grid=(n // bn, m // bm),
        in_specs=[
            pl.BlockSpec((bm, kh), lambda j, i: (i, 0)),
            pl.BlockSpec((bm, kh), lambda j, i: (i, 0)),
            pl.BlockSpec((kh, bn), lambda j, i: (0, j), pipeline_mode=pl.Buffered(1)),
            pl.BlockSpec((kh, bn), lambda j, i: (1, j), pipeline_mode=pl.Buffered(1)),
            pl.BlockSpec((bm, bn), lambda j, i: (i, j)),
            pl.BlockSpec((1, bn), lambda j, i: (0, j)),
        ],
        out_specs=[pl.BlockSpec((bm, bn), lambda j, i: (i, j)),
                   pl.BlockSpec((bm, bn), lambda j, i: (i, j)),
                   pl.BlockSpec((None, bm, LANES), lambda j, i: (j, i, 0))],
        out_shape=[jax.ShapeDtypeStruct((m, n), F32),
                   jax.ShapeDtypeStruct((m, n), BF16),
                   jax.ShapeDtypeStruct((n // bn, m, LANES), F32)],
        compiler_params=_cparams(("parallel", "parallel"), 56),
        name="outproj",
    )(na_o, sw_o, w, w, resid, gain)


def _qkt(q, k):
    return lax.dot_general(q, k, (((1,), (1,)), ((), ())), preferred_element_type=F32)


def _na_plan():
    vblk = 6
    vrows = vblk * NA_QROWS
    specs, plans = [], []
    for j in (0, 2, vblk - 1):
        ws = int(np.clip(j - 1, 0, vblk - NA_KBLKS))
        plan = []
        for qrl in range(NA_QROWS):
            qr = NA_QROWS * j + qrl
            rs = int(np.clip(qr - NA_ROWS // 2, 0, vrows - NA_ROWS))
            row = []
            for t in range(NA_TILES_PER_ROW):
                kr0 = NA_QROWS * ws + 2 * t
                halves = tuple(kr - qr if rs <= kr < rs + NA_ROWS else None for kr in (kr0, kr0 + 1))
                if halves == (None, None):
                    row.append(-1)
                else:
                    if halves not in specs:
                        specs.append(halves)
                    row.append(specs.index(halves))
            plan.append(row)
        plans.append(plan)
    return specs, plans


def _na_tiles(rpb, specs):
    nh = rpb.shape[0]
    c = np.arange(GRID_W)
    cs = np.clip(c - NA_COLS // 2, 0, GRID_W - NA_COLS)
    col_in = (c[None, :] >= cs[:, None]) & (c[None, :] < cs[:, None] + NA_COLS)
    dc = np.clip(c[None, :] - c[:, None], -(NA_COLS - 1), NA_COLS - 1) + (NA_COLS - 1)
    onehot = np.zeros((2 * NA_COLS - 1, GRID_W * GRID_W), np.float32)
    onehot[dc.reshape(-1), np.arange(GRID_W * GRID_W)] = 1.0
    tcol = jnp.einsum("hab,bx->hax", rpb.astype(F32), onehot, precision=lax.Precision.HIGHEST)
    tcol = jnp.where(col_in[None, None], tcol.reshape(nh, 2 * NA_ROWS - 1, GRID_W, GRID_W), NEG_INF)
    masked = jnp.full((nh, GRID_W, GRID_W), NEG_INF, F32)

    def half(dr):
        return masked if dr is None else tcol[:, dr + NA_ROWS - 1]

    tiles = jnp.stack([jnp.concatenate([half(l), half(r)], axis=-1) for l, r in specs], axis=1)
    return tiles * LOG2_E


def _na_kernel(q_ref, k0_ref, k1_ref, k2_ref, v0_ref, v1_ref, v2_ref, km_ref, vm_ref, tiles_ref,
               o_ref, bias_ref, *, scale, plans):
    hb = q_ref.shape[0]
    j = pl.program_id(2)
    last = pl.num_programs(2) - 1
    lane = lax.broadcasted_iota(jnp.int32, (GRID_W, LANES), 1)
    meta_tile = jnp.where(lane < N_META, 0.0, NEG_INF).astype(F32)
    masked_tile = jnp.full((GRID_W, LANES), NEG_INF, F32)

    def build(plan):
        def body():
            for hh in range(hb):
                for qrl in range(NA_QROWS):
                    for t in range(NA_TILES_PER_ROW + 1):
                        if t == NA_TILES_PER_ROW:
                            tile = meta_tile
                        else:
                            idx = plan[qrl][t]
                            tile = masked_tile if idx < 0 else tiles_ref[hh, idx]
                        bias_ref[hh, qrl * GRID_W:(qrl + 1) * GRID_W, t * LANES:(t + 1) * LANES] = tile
        return body

    pl.when(j == 0)(build(plans[0]))
    pl.when(j == 1)(build(plans[1]))
    pl.when(j == last)(build(plans[2]))

    for hh in range(hb):
        q = q_ref[hh]
        k = jnp.concatenate([k0_ref[hh], k1_ref[hh], k2_ref[hh], km_ref[hh]], axis=0)
        v = jnp.concatenate([v0_ref[hh], v1_ref[hh], v2_ref[hh], vm_ref[hh]], axis=0)
        s = _qkt(q, k) * (scale * LOG2_E) + bias_ref[hh]
        p = jnp.exp2(s - jnp.max(s, axis=-1, keepdims=True))
        den = jnp.sum(p, axis=-1, keepdims=True)
        o = jnp.dot(p.astype(BF16), v, preferred_element_type=F32)
        o_ref[:, hh * LANES:(hh + 1) * LANES] = (o / den).astype(BF16)


def _na_attention(proj, km, vm, tiles, plans, batch, rows):
    nh = km.shape[0]
    hb = NA_HEADS_PER_STEP
    nblk = rows // NA_QROWS
    tokens = proj.shape[1]
    assert nblk >= NA_KBLKS and nh % hb == 0
    hblks = nh // hb

    def kv_map(base, d):
        def f(h, b, j):
            ws = jnp.clip(j - 1, 0, nblk - NA_KBLKS)
            return (base + h, b * nblk + ws + d, 0)
        return f

    qspec = pl.BlockSpec((hb, NA_QBLK, LANES), lambda h, b, j: (h, b * nblk + j, 0))
    kspecs = [pl.BlockSpec((hb, NA_QBLK, LANES), kv_map(hblks, d)) for d in range(NA_KBLKS)]
    vspecs = [pl.BlockSpec((hb, NA_QBLK, LANES), kv_map(2 * hblks, d)) for d in range(NA_KBLKS)]
    mspec = pl.BlockSpec((hb, LANES, LANES), lambda h, b, j: (h, 0, 0))
    tspec = pl.BlockSpec((hb,) + tiles.shape[1:], lambda h, b, j: (h, 0, 0, 0))
    return pl.pallas_call(
        functools.partial(_na_kernel, scale=HEAD_DIM ** -0.5, plans=plans),
        grid=(hblks, batch, nblk),
        in_specs=[qspec] + kspecs + vspecs + [mspec, mspec, tspec],
        out_specs=pl.BlockSpec((NA_QBLK, hb * LANES), lambda h, b, j: (b * nblk + j, h)),
        out_shape=jax.ShapeDtypeStruct((tokens, nh * LANES), BF16),
        scratch_shapes=[pltpu.VMEM((hb, NA_QBLK, NA_KBLKS * NA_QBLK + LANES), F32)],
        compiler_params=_cparams(("arbitrary", "arbitrary", "arbitrary"), 48),
        name="na_attn",
    )(proj, proj, proj, proj, proj, proj, proj, km, vm, tiles)


def _swa_kernel(q_ref, k0_ref, k1_ref, k2_ref, v0_ref, v1_ref, v2_ref, km_ref, vm_ref, bias_ref,
                o_ref, *, scale, group):
    hkb = k0_ref.shape[0]
    blk = q_ref.shape[1]
    for kh in range(hkb):
        heads = slice(kh * group, (kh + 1) * group)
        q = q_ref[heads].reshape(group * blk, LANES)
        k = jnp.concatenate([k0_ref[kh], k1_ref[kh], k2_ref[kh], km_ref[kh]], axis=0)
        v = jnp.concatenate([v0_ref[kh], v1_ref[kh], v2_ref[kh], vm_ref[kh]], axis=0)
        s = _qkt(q, k) * (scale * LOG2_E) + bias_ref[heads].reshape(group * blk, 3 * blk + LANES)
        p = jnp.exp2(s - jnp.max(s, axis=-1, keepdims=True))
        den = jnp.sum(p, axis=-1, keepdims=True)
        o = (jnp.dot(p.astype(BF16), v, preferred_element_type=F32) / den).astype(BF16)
        for g in range(group):
            col = (kh * group + g) * LANES
            o_ref[:, col:col + LANES] = o[g * blk:(g + 1) * blk]


def _swa_attention(proj, km, vm, bias, batch, seq, q_base, k_base, v_base):
    hkv = km.shape[0]
    hq = bias.shape[1]
    group = hq // hkv
    hkb = SWA_KV_HEADS_PER_STEP
    blk = SWA_BLOCK
    nb = seq // blk
    tokens = proj.shape[1]
    assert hkv % hkb == 0 and k_base % hkb == 0 and v_base % hkb == 0 and q_base % (hkb * group) == 0

    def bias_map(hk, b, i):
        variant = jnp.where(i == 0, 1, 0) + jnp.where(i == nb - 1, 2, 0)
        return (variant, hk, 0, 0)

    def kv_map(base, d):
        def f(hk, b, i):
            return (base // hkb + hk, b * nb + jnp.clip(i - 1 + d, 0, nb - 1), 0)
        return f

    kspecs = [pl.BlockSpec((hkb, blk, LANES), kv_map(k_base, d)) for d in range(3)]
    vspecs = [pl.BlockSpec((hkb, blk, LANES), kv_map(v_base, d)) for d in range(3)]
    mspec = pl.BlockSpec((hkb, LANES, LANES), lambda hk, b, i: (hk, 0, 0))
    return pl.pallas_call(
        functools.partial(_swa_kernel, scale=HEAD_DIM ** -0.5, group=group),
        grid=(hkv // hkb, batch, nb),
        in_specs=[pl.BlockSpec((hkb * group, blk, LANES),
                               lambda hk, b, i: (q_base // (hkb * group) + hk, b * nb + i, 0))]
                 + kspecs + vspecs + [mspec, mspec,
                  pl.BlockSpec((None, hkb * group, blk, 3 * blk + LANES), bias_map)],
        out_specs=pl.BlockSpec((blk, hkb * group * LANES), lambda hk, b, i: (b * nb + i, hk)),
        out_shape=jax.ShapeDtypeStruct((tokens, hq * LANES), BF16),
        compiler_params=_cparams(("parallel", "parallel", "arbitrary"), 32),
        name="swa_attn",
    )(proj, proj, proj, proj, proj, proj, proj, km, vm, bias)


def _t5_bucket(rel):
    nb = T5_BUCKETS // 2
    max_exact = nb // 2
    ret = np.where(rel > 0, nb, 0)
    n = np.abs(rel)
    large = max_exact + (np.log(np.maximum(n, 1) / max_exact)
                         / math.log(T5_MAX_DIST / max_exact) * (nb - max_exact)).astype(np.int64)
    large = np.minimum(large, nb - 1)
    return ret + np.where(n < max_exact, n, large)


def _select_rows(table, idx):
    flat = idx.reshape(-1)
    onehot = np.zeros((table.shape[0], flat.size), np.float32)
    onehot[flat, np.arange(flat.size)] = 1.0
    out = jnp.einsum("bh,bx->xh", table.astype(F32), onehot, precision=lax.Precision.HIGHEST)
    return out.reshape(idx.shape + (table.shape[1],))


def _swa_tables(t5_bias, sink):
    blk = SWA_BLOCK
    rel = np.arange(-(2 * blk - 1), 2 * blk)
    by_rel = _select_rows(t5_bias, _t5_bucket(rel)).T
    by_rel = jnp.where((np.abs(rel) <= SWA_WINDOW)[None], by_rel, NEG_INF)
    bias_w = jnp.stack([by_rel[:, blk - 1 - q:blk - 1 - q + 3 * blk] for q in range(blk)], axis=1)
    jj = np.arange(3 * blk)
    edge = np.stack([np.zeros_like(jj, bool), jj < blk, jj >= 2 * blk, (jj < blk) | (jj >= 2 * blk)])
    bias_w = jnp.where(edge[:, None, None, :], NEG_INF, bias_w[None])
    assert N_META + blk - (N_META - 1) > T5_MAX_DIST
    rel_m = np.arange(N_META)[None, :] - (N_META + np.arange(2 * blk))[:, None]
    bias_m = _select_rows(t5_bias, _t5_bucket(rel_m))
    bias_m = bias_m.reshape(2, blk, N_META, -1).transpose(0, 3, 1, 2)
    bias_m = jnp.stack([bias_m[1], bias_m[0], bias_m[1], bias_m[0]])
    hq = bias_m.shape[1]
    sink_col = jnp.broadcast_to(sink.astype(F32)[None, :, None, None], (4, hq, blk, 1))
    pad = jnp.full((4, hq, blk, LANES - N_META - 1), NEG_INF, F32)
    return jnp.concatenate([bias_w, bias_m, sink_col, pad], axis=-1) * LOG2_E


def _trunk(x, meta_kv, tabs, params):
    batch, seq, d = x.shape
    (g_attn, w_in, head_gain, head_flag, w_out, g_mlp, w_up, w_down) = params
    km_na, vm_na, km_sw, vm_sw = meta_kv
    na_tiles, na_plans, swa_bias = tabs
    nh_na = km_na.shape[0]
    hq_sw = swa_bias.shape[1]
    hkv_sw = km_sw.shape[0]
    x2 = x.reshape(batch * seq, d)
    proj = _inproj(x2, g_attn, w_in, head_gain, head_flag, 512, 1024)
    rows = seq // GRID_W
    assert seq % (GRID_W * NA_QROWS) == 0 and rows >= NA_ROWS
    assert seq % SWA_BLOCK == 0
    na_o = _na_attention(proj, km_na, vm_na, na_tiles, na_plans, batch, rows)
    q_base = 3 * nh_na
    k_base = q_base + hq_sw
    v_base = k_base + hkv_sw
    sw_o = _swa_attention(proj, km_sw, vm_sw, swa_bias, batch, seq, q_base, k_base, v_base)
    x1, x1g, x1ss = _outproj(na_o, sw_o, w_out, x2, g_mlp, 512, 2048)
    u = _mlp_up(x1g, x1ss, w_up, 1024, 1024)
    y = _mlp_down(u, w_down, x1, 1024, 1024, 2048)
    return y.reshape(batch, seq, d)


def kernel(x_prompt, x_sample, meta_tokens, t5_bias, norm_attn, w_in, q_norm_na, k_norm_na, na_rpb,
           q_norm_swa, k_norm_swa, swa_sink, w_out, norm_mlp, w_up, w_down):
    depth = w_in.shape[0]
    assert depth == 1
    nh_na = na_rpb.shape[1]
    hq_sw = swa_sink.shape[1]
    in_width = w_in.shape[2]
    hkv_sw = (in_width // HEAD_DIM - 3 * nh_na - hq_sw) // 2

    ones = jnp.ones((HEAD_DIM,), F32)

    def rep(g, n):
        return jnp.broadcast_to(g.astype(F32)[None], (n, HEAD_DIM))

    head_gain = jnp.concatenate([
        rep(q_norm_na[0], nh_na), rep(k_norm_na[0], nh_na), rep(ones, nh_na),
        rep(q_norm_swa[0], hq_sw), rep(k_norm_swa[0], hkv_sw), rep(ones, hkv_sw)])[:, None, :]
    flag = np.concatenate([np.ones(2 * nh_na), np.zeros(nh_na), np.ones(hq_sw + hkv_sw),
                           np.zeros(hkv_sw)]).astype(np.float32)
    head_flag = jnp.asarray(np.broadcast_to(flag[:, None, None], (flag.shape[0], 1, HEAD_DIM)))

    w_in_b = w_in[0].astype(BF16)
    w_out_b = w_out[0].astype(BF16)
    w_up_f = w_up[0].astype(F32)
    w_down_b = w_down[0].astype(BF16)
    g_attn = norm_attn[0].astype(F32)[None]
    g_mlp = norm_mlp[0].astype(F32)[None]

    mproj = _inproj(meta_tokens.astype(F32), g_attn, w_in_b, head_gain, head_flag, N_META, 512)
    mproj = jnp.pad(mproj, ((0, 0), (0, LANES - N_META), (0, 0)))
    k0 = 3 * nh_na + hq_sw
    meta_kv = (mproj[nh_na:2 * nh_na], mproj[2 * nh_na:3 * nh_na],
               mproj[k0:k0 + hkv_sw], mproj[k0 + hkv_sw:k0 + 2 * hkv_sw])

    specs, plans = _na_plan()
    tabs = (_na_tiles(na_rpb[0], specs), plans, _swa_tables(t5_bias, swa_sink[0]))
    params = (g_attn, w_in_b, head_gain, head_flag, w_out_b, g_mlp, w_up_f, w_down_b)
    y_prompt = _trunk(x_prompt, meta_kv, tabs, params)
    y_sample = _trunk(x_sample, meta_kv, tabs, params)
    return (y_prompt, y_sample)
```

```python
import functools
import math

import jax
import jax.numpy as jnp
import numpy as np
from jax import lax
from jax.experimental import pallas as pl
from jax.experimental.pallas import tpu as pltpu

F32 = jnp.float32
BF16 = jnp.bfloat16

HEAD_DIM = 128
N_META = 16
GRID_W = 64
NA_ROWS = 8
NA_COLS = 16
SWA_WINDOW = 128
SWA_BLOCK = 128
T5_BUCKETS = 32
T5_MAX_DIST = 128
NORM_EPS = 1e-6
NEG_INF = -1e30
LOG2_E = math.log2(math.e)

LANES = 128
NA_QROWS = 4
NA_QBLK = NA_QROWS * GRID_W
NA_KBLKS = 3
NA_HEADS_PER_STEP = 8
SWA_KV_HEADS_PER_STEP = 4
NA_TILES_PER_ROW = NA_KBLKS * NA_QBLK // LANES
MIB = 1024 * 1024

assert 2 * GRID_W == LANES


def _cparams(sem, vmem_mib):
    return pltpu.CompilerParams(dimension_semantics=sem, vmem_limit_bytes=vmem_mib * MIB)


def _rms_rows(x, gain):
    ms = jnp.mean(x * x, axis=-1, keepdims=True)
    return x * lax.rsqrt(ms + NORM_EPS) * gain


def _inproj_kernel(x_ref, g_ref, w_ref, hg_ref, hf_ref, o_ref, xn_ref):
    @pl.when(pl.program_id(1) == 0)
    def _():
        xn_ref[...] = _rms_rows(x_ref[...], g_ref[...]).astype(BF16)

    res = jnp.dot(xn_ref[...], w_ref[...], preferred_element_type=F32)
    for c in range(o_ref.shape[0]):
        blk = res[:, c * LANES:(c + 1) * LANES]
        r = lax.rsqrt(jnp.mean(blk * blk, axis=-1, keepdims=True) + NORM_EPS)
        f = hf_ref[c]
        o_ref[c] = (blk * (f * r + (1.0 - f)) * hg_ref[c]).astype(BF16)


def _inproj(x2, gain, w, head_gain, head_flag, bm, bn):
    m, d = x2.shape
    n = w.shape[1]
    cpb = bn // LANES
    return pl.pallas_call(
        _inproj_kernel,
        grid=(m // bm, n // bn),
        in_specs=[
            pl.BlockSpec((bm, d), lambda i, j: (i, 0)),
            pl.BlockSpec((1, d), lambda i, j: (0, 0)),
            pl.BlockSpec((d, bn), lambda i, j: (0, j)),
            pl.BlockSpec((cpb, 1, LANES), lambda i, j: (j, 0, 0)),
            pl.BlockSpec((cpb, 1, LANES), lambda i, j: (j, 0, 0)),
        ],
        out_specs=pl.BlockSpec((cpb, bm, LANES), lambda i, j: (j, i, 0)),
        out_shape=jax.ShapeDtypeStruct((n // LANES, m, LANES), BF16),
        scratch_shapes=[pltpu.VMEM((bm, d), BF16)],
        compiler_params=_cparams(("parallel", "arbitrary"), 56),
        name="inproj",
    )(x2, gain, w, head_gain, head_flag)


def _mlp_up_kernel(a_ref, ss_ref, w_ref, o_ref, *, d):
    ss = ss_ref[0][:, :1]
    for c in range(1, ss_ref.shape[0]):
        ss = ss + ss_ref[c][:, :1]
    r = lax.rsqrt(ss * (1.0 / d) + NORM_EPS)
    u = jnp.dot(a_ref[...], w_ref[...], preferred_element_type=F32) * r
    u = jnp.maximum(u, 0.0)
    o_ref[...] = (u * u).astype(BF16)


def _mlp_up(a, ss, w, bm, bn):
    m, d = a.shape
    n = w.shape[1]
    ncol = ss.shape[0]
    return pl.pallas_call(
        functools.partial(_mlp_up_kernel, d=d),
        grid=(m // bm, n // bn),
        in_specs=[
            pl.BlockSpec((bm, d), lambda i, j: (i, 0), pipeline_mode=pl.Buffered(1)),
            pl.BlockSpec((ncol, bm, LANES), lambda i, j: (0, i, 0)),
            pl.BlockSpec((d, bn), lambda i, j: (0, j)),
        ],
        out_specs=pl.BlockSpec((bm, bn), lambda i, j: (i, j)),
        out_shape=jax.ShapeDtypeStruct((m, n), BF16),
        compiler_params=_cparams(("parallel", "arbitrary"), 56),
        name="mlp_up",
    )(a, ss, w)


def _mlp_down_kernel(a_ref, w_ref, r_ref, o_ref):
    @pl.when(pl.program_id(2) == 0)
    def _():
        o_ref[...] = r_ref[...]

    o_ref[...] += jnp.dot(a_ref[...], w_ref[...], preferred_element_type=F32)


def _mlp_down(a, w, resid, bm, bn, bk):
    m, kd = a.shape
    n = w.shape[1]
    return pl.pallas_call(
        _mlp_down_kernel,
        grid=(m // bm, n // bn, kd // bk),
        in_specs=[
            pl.BlockSpec((bm, bk), lambda i, j, k: (i, k)),
            pl.BlockSpec((bk, bn), lambda i, j, k: (k, j)),
            pl.BlockSpec((bm, bn), lambda i, j, k: (i, j), pipeline_mode=pl.Buffered(1)),
        ],
        out_specs=pl.BlockSpec((bm, bn), lambda i, j, k: (i, j)),
        out_shape=jax.ShapeDtypeStruct((m, n), F32),
        compiler_params=_cparams(("parallel", "parallel", "arbitrary"), 52),
        name="mlp_down",
    )(a, w, resid)


def _outproj_kernel(na_ref, sw_ref, wt_ref, wb_ref, r_ref, g_ref, o_ref, ob_ref, ss_ref):
    acc = jnp.dot(na_ref[...], wt_ref[...], preferred_element_type=F32)
    acc = acc + jnp.dot(sw_ref[...], wb_ref[...], preferred_element_type=F32)
    x1 = r_ref[...] + acc
    o_ref[...] = x1
    ob_ref[...] = (x1 * g_ref[...]).astype(BF16)
    ss_ref[...] = jnp.broadcast_to(jnp.sum(x1 * x1, axis=-1, keepdims=True), ss_ref.shape)


def _outproj(na_o, sw_o, w, resid, gain, bm, bn):
    m, kh = na_o.shape
    n = w.shape[1]
    return pl.pallas_call(
        _outproj_kernel,
        grid=(n // bn, m // bm),
        in_specs=[
            pl.BlockSpec((bm, kh), lambda j, i: (i, 0)),
            pl.BlockSpec((bm, kh), lambda j, i: (i, 0)),
            pl.BlockSpec((kh, bn), lambda j, i: (0, j), pipeline_mode=pl.Buffered(1)),
            pl.BlockSpec((kh, bn), lambda j, i: (1, j), pipeline_mode=pl.Buffered(1)),
            pl.BlockSpec((bm, bn), lambda j, i: (i, j)),
            pl.BlockSpec((1, bn), lambda j, i: (0, j)),
        ],
        out_specs=[pl.BlockSpec((bm, bn), lambda j, i: (i, j)),
                   pl.BlockSpec((bm, bn), lambda j, i: (i, j)),
                   pl.BlockSpec((None, bm, LANES), lambda j, i: (j, i, 0))],
        out_shape=[jax.ShapeDtypeStruct((m, n), F32),
                   jax.ShapeDtypeStruct((m, n), BF16),
                   jax.ShapeDtypeStruct((n // bn, m, LANES), F32)],
        compiler_params=_cparams(("parallel", "parallel"), 56),
        name="outproj",
    )(na_o, sw_o, w, w, resid, gain)


def _qkt(q, k):
    return lax.dot_general(q, k, (((1,), (1,)), ((), ())), preferred_element_type=F32)


def _na_plan():
    vblk = 6
    vrows = vblk * NA_QROWS
    wrows = NA_KBLKS * NA_QROWS
    specs, plans = [], []
    for j in (0, 2, vblk - 1):
        ws = int(np.clip(j - 1, 0, vblk - NA_KBLKS))
        meta_row = 0 if j == vblk - 1 else wrows - 1
        plan = []
        for qrl in range(NA_QROWS):
            qr = NA_QROWS * j + qrl
            rs = int(np.clip(qr - NA_ROWS // 2, 0, vrows - NA_ROWS))
            row = []
            for t in range(NA_TILES_PER_ROW):
                halves = []
                for krl in (2 * t, 2 * t + 1):
                    kr = NA_QROWS * ws + krl
                    half = kr - qr if rs <= kr < rs + NA_ROWS else None
                    if krl == meta_row:
                        assert half is None
                        half = "meta"
                    halves.append(half)
                halves = tuple(halves)
                if halves == (None, None):
                    row.append(-1)
                else:
                    if halves not in specs:
                        specs.append(halves)
                    row.append(specs.index(halves))
            plan.append(row)
        plans.append(plan)
    return specs, plans


def _na_tiles(rpb, specs):
    nh = rpb.shape[0]
    c = np.arange(GRID_W)
    cs = np.clip(c - NA_COLS // 2, 0, GRID_W - NA_COLS)
    col_in = (c[None, :] >= cs[:, None]) & (c[None, :] < cs[:, None] + NA_COLS)
    dc = np.clip(c[None, :] - c[:, None], -(NA_COLS - 1), NA_COLS - 1) + (NA_COLS - 1)
    onehot = np.zeros((2 * NA_COLS - 1, GRID_W * GRID_W), np.float32)
    onehot[dc.reshape(-1), np.arange(GRID_W * GRID_W)] = 1.0
    tcol = jnp.einsum("hab,bx->hax", rpb.astype(F32), onehot, precision=lax.Precision.HIGHEST)
    tcol = jnp.where(col_in[None, None], tcol.reshape(nh, 2 * NA_ROWS - 1, GRID_W, GRID_W), NEG_INF)
    masked = jnp.full((nh, GRID_W, GRID_W), NEG_INF, F32)
    meta = jnp.where(np.arange(GRID_W) < N_META, 0.0, masked)

    def half(dr):
        if dr is None:
            return masked
        return meta if dr == "meta" else tcol[:, dr + NA_ROWS - 1]

    tiles = jnp.stack([jnp.concatenate([half(l), half(r)], axis=-1) for l, r in specs], axis=1)
    return tiles * LOG2_E


def _na_kernel(q_ref, k0_ref, k1_ref, k2_ref, v0_ref, v1_ref, v2_ref, km_ref, vm_ref, tiles_ref,
               o_ref, bias_ref, k_scr, v_scr, *, scale, plans):
    hb = q_ref.shape[0]
    j = pl.program_id(2)
    last = pl.num_programs(2) - 1
    masked_tile = jnp.full((GRID_W, LANES), NEG_INF, F32)

    def build(plan):
        def body():
            for hh in range(hb):
                for qrl in range(NA_QROWS):
                    for t in range(NA_TILES_PER_ROW):
                        idx = plan[qrl][t]
                        tile = masked_tile if idx < 0 else tiles_ref[hh, idx]
                        bias_ref[hh, qrl * GRID_W:(qrl + 1) * GRID_W, t * LANES:(t + 1) * LANES] = tile
        return body

    pl.when(j == 0)(build(plans[0]))
    pl.when(j == 1)(build(plans[1]))
    pl.when(j == last)(build(plans[2]))

    meta_pos = pl.multiple_of(
        jnp.where(j == last, 0, (NA_KBLKS * NA_QROWS - 1) * GRID_W), N_META)
    for hh in range(hb):
        for d, (kb_ref, vb_ref) in enumerate(((k0_ref, v0_ref), (k1_ref, v1_ref), (k2_ref, v2_ref))):
            k_scr[hh, d * NA_QBLK:(d + 1) * NA_QBLK, :] = kb_ref[hh]
            v_scr[hh, d * NA_QBLK:(d + 1) * NA_QBLK, :] = vb_ref[hh]
        k_scr[hh, pl.ds(meta_pos, N_META), :] = km_ref[hh]
        v_scr[hh, pl.ds(meta_pos, N_META), :] = vm_ref[hh]
        q = q_ref[hh]
        k = k_scr[hh]
        v = v_scr[hh]
        s = _qkt(q, k) * (scale * LOG2_E) + bias_ref[hh]
        p = jnp.exp2(s - jnp.max(s, axis=-1, keepdims=True))
        den = jnp.sum(p, axis=-1, keepdims=True)
        o = jnp.dot(p.astype(BF16), v, preferred_element_type=F32)
        o_ref[:, hh * LANES:(hh + 1) * LANES] = (o / den).astype(BF16)


def _na_attention(proj, km, vm, tiles, plans, batch, rows):
    nh = km.shape[0]
    hb = NA_HEADS_PER_STEP
    nblk = rows // NA_QROWS
    tokens = proj.shape[1]
    assert nblk >= NA_KBLKS and nh % hb == 0
    hblks = nh // hb

    def kv_map(base, d):
        def f(h, b, j):
            ws = jnp.clip(j - 1, 0, nblk - NA_KBLKS)
            return (base + h, b * nblk + ws + d, 0)
        return f

    qspec = pl.BlockSpec((hb, NA_QBLK, LANES), lambda h, b, j: (h, b * nblk + j, 0))
    kspecs = [pl.BlockSpec((hb, NA_QBLK, LANES), kv_map(hblks, d)) for d in range(NA_KBLKS)]
    vspecs = [pl.BlockSpec((hb, NA_QBLK, LANES), kv_map(2 * hblks, d)) for d in range(NA_KBLKS)]
    mspec = pl.BlockSpec((hb, N_META, LANES), lambda h, b, j: (h, 0, 0))
    tspec = pl.BlockSpec((hb,) + tiles.shape[1:], lambda h, b, j: (h, 0, 0, 0))
    return pl.pallas_call(
        functools.partial(_na_kernel, scale=HEAD_DIM ** -0.5, plans=plans),
        grid=(hblks, batch, nblk),
        in_specs=[qspec] + kspecs + vspecs + [mspec, mspec, tspec],
        out_specs=pl.BlockSpec((NA_QBLK, hb * LANES), lambda h, b, j: (b * nblk + j, h)),
        out_shape=jax.ShapeDtypeStruct((tokens, nh * LANES), BF16),
        scratch_shapes=[pltpu.VMEM((hb, NA_QBLK, NA_KBLKS * NA_QBLK), F32),
                        pltpu.VMEM((hb, NA_KBLKS * NA_QBLK, LANES), BF16),
                        pltpu.VMEM((hb, NA_KBLKS * NA_QBLK, LANES), BF16)],
        compiler_params=_cparams(("arbitrary", "arbitrary", "arbitrary"), 48),
        name="na_attn",
    )(proj, proj, proj, proj, proj, proj, proj, km, vm, tiles)


def _swa_kernel(q_ref, k0_ref, k1_ref, k2_ref, v0_ref, v1_ref, v2_ref, km_ref, vm_ref, bias_ref,
                o_ref, *, scale, group):
    hkb = k0_ref.shape[0]
    blk = q_ref.shape[1]
    for kh in range(hkb):
        heads = slice(kh * group, (kh + 1) * group)
        q = q_ref[heads].reshape(group * blk, LANES)
        k = jnp.concatenate([k0_ref[kh], k1_ref[kh], k2_ref[kh], km_ref[kh]], axis=0)
        v = jnp.concatenate([v0_ref[kh], v1_ref[kh], v2_ref[kh], vm_ref[kh]], axis=0)
        s = _qkt(q, k) * (scale * LOG2_E) + bias_ref[heads].reshape(group * blk, 3 * blk + LANES)
        p = jnp.exp2(s - jnp.max(s, axis=-1, keepdims=True))
        den = jnp.sum(p, axis=-1, keepdims=True)
        o = (jnp.dot(p.astype(BF16), v, preferred_element_type=F32) / den).astype(BF16)
        for g in range(group):
            col = (kh * group + g) * LANES
            o_ref[:, col:col + LANES] = o[g * blk:(g + 1) * blk]


def _swa_attention(proj, km, vm, bias, batch, seq, q_base, k_base, v_base):
    hkv = km.shape[0]
    hq = bias.shape[1]
    group = hq // hkv
    hkb = SWA_KV_HEADS_PER_STEP
    blk = SWA_BLOCK
    nb = seq // blk
    tokens = proj.shape[1]
    assert hkv % hkb == 0 and k_base % hkb == 0 and v_base % hkb == 0 and q_base % (hkb * group) == 0

    def bias_map(hk, b, i):
        variant = jnp.where(i == 0, 1, 0) + jnp.where(i == nb - 1, 2, 0)
        return (variant, hk, 0, 0)

    def kv_map(base, d):
        def f(hk, b, i):
            return (base // hkb + hk, b * nb + jnp.clip(i - 1 + d, 0, nb - 1), 0)
        return f

    kspecs = [pl.BlockSpec((hkb, blk, LANES), kv_map(k_base, d)) for d in range(3)]
    vspecs = [pl.BlockSpec((hkb, blk, LANES), kv_map(v_base, d)) for d in range(3)]
    mspec = pl.BlockSpec((hkb, LANES, LANES), lambda hk, b, i: (hk, 0, 0))
    return pl.pallas_call(
        functools.partial(_swa_kernel, scale=HEAD_DIM ** -0.5, group=group),
        grid=(hkv // hkb, batch, nb),
        in_specs=[pl.BlockSpec((hkb * group, blk, LANES),
                               lambda hk, b, i: (q_base // (hkb * group) + hk, b * nb + i, 0))]
                 + kspecs + vspecs + [mspec, mspec,
                  pl.BlockSpec((None, hkb * group, blk, 3 * blk + LANES), bias_map)],
        out_specs=pl.BlockSpec((blk, hkb * group * LANES), lambda hk, b, i: (b * nb + i, hk)),
        out_shape=jax.ShapeDtypeStruct((tokens, hq * LANES), BF16),
        compiler_params=_cparams(("parallel", "parallel", "arbitrary"), 32),
        name="swa_attn",
    )(proj, proj, proj, proj, proj, proj, proj, km, vm, bias)


def _t5_bucket(rel):
    nb = T5_BUCKETS // 2
    max_exact = nb // 2
    ret = np.where(rel > 0, nb, 0)
    n = np.abs(rel)
    large = max_exact + (np.log(np.maximum(n, 1) / max_exact)
                         / math.log(T5_MAX_DIST / max_exact) * (nb - max_exact)).astype(np.int64)
    large = np.minimum(large, nb - 1)
    return ret + np.where(n < max_exact, n, large)


def _select_rows(table, idx):
    flat = idx.reshape(-1)
    onehot = np.zeros((table.shape[0], flat.size), np.float32)
    onehot[flat, np.arange(flat.size)] = 1.0
    out = jnp.einsum("bh,bx->xh", table.astype(F32), onehot, precision=lax.Precision.HIGHEST)
    return out.reshape(idx.shape + (table.shape[1],))


def _swa_tables(t5_bias, sink):
    blk = SWA_BLOCK
    rel = np.arange(-(2 * blk - 1), 2 * blk)
    by_rel = _select_rows(t5_bias, _t5_bucket(rel)).T
    by_rel = jnp.where((np.abs(rel) <= SWA_WINDOW)[None], by_rel, NEG_INF)
    bias_w = jnp.stack([by_rel[:, blk - 1 - q:blk - 1 - q + 3 * blk] for q in range(blk)], axis=1)
    jj = np.arange(3 * blk)
    edge = np.stack([np.zeros_like(jj, bool), jj < blk, jj >= 2 * blk, (jj < blk) | (jj >= 2 * blk)])
    bias_w = jnp.where(edge[:, None, None, :], NEG_INF, bias_w[None])
    assert N_META + blk - (N_META - 1) > T5_MAX_DIST
    rel_m = np.arange(N_META)[None, :] - (N_META + np.arange(2 * blk))[:, None]
    bias_m = _select_rows(t5_bias, _t5_bucket(rel_m))
    bias_m = bias_m.reshape(2, blk, N_META, -1).transpose(0, 3, 1, 2)
    bias_m = jnp.stack([bias_m[1], bias_m[0], bias_m[1], bias_m[0]])
    hq = bias_m.shape[1]
    sink_col = jnp.broadcast_to(sink.astype(F32)[None, :, None, None], (4, hq, blk, 1))
    pad = jnp.full((4, hq, blk, LANES - N_META - 1), NEG_INF, F32)
    return jnp.concatenate([bias_w, bias_m, sink_col, pad], axis=-1) * LOG2_E


def _trunk(x, meta_kv, tabs, params):
    batch, seq, d = x.shape
    (g_attn, w_in, head_gain, head_flag, w_out, g_mlp, w_up, w_down) = params
    km_na, vm_na, km_sw, vm_sw = meta_kv
    na_tiles, na_plans, swa_bias = tabs
    nh_na = km_na.shape[0]
    hq_sw = swa_bias.shape[1]
    hkv_sw = km_sw.shape[0]
    x2 = x.reshape(batch * seq, d)
    proj = _inproj(x2, g_attn, w_in, head_gain, head_flag, 512, 1024)
    rows = seq // GRID_W
    assert seq % (GRID_W * NA_QROWS) == 0 and rows >= NA_ROWS
    assert seq % SWA_BLOCK == 0
    na_o = _na_attention(proj, km_na, vm_na, na_tiles, na_plans, batch, rows)
    q_base = 3 * nh_na
    k_base = q_base + hq_sw
    v_base = k_base + hkv_sw
    sw_o = _swa_attention(proj, km_sw, vm_sw, swa_bias, batch, seq, q_base, k_base, v_base)
    x1, x1g, x1ss = _outproj(na_o, sw_o, w_out, x2, g_mlp, 512, 2048)
    u = _mlp_up(x1g, x1ss, w_up, 1024, 1024)
    y = _mlp_down(u, w_down, x1, 2048, 1024, 1024)
    return y.reshape(batch, seq, d)


def kernel(x_prompt, x_sample, meta_tokens, t5_bias, norm_attn, w_in, q_norm_na, k_norm_na, na_rpb,
           q_norm_swa, k_norm_swa, swa_sink, w_out, norm_mlp, w_up, w_down):
    depth = w_in.shape[0]
    assert depth == 1
    nh_na = na_rpb.shape[1]
    hq_sw = swa_sink.shape[1]
    in_width = w_in.shape[2]
    hkv_sw = (in_width // HEAD_DIM - 3 * nh_na - hq_sw) // 2

    ones = jnp.ones((HEAD_DIM,), F32)

    def rep(g, n):
        return jnp.broadcast_to(g.astype(F32)[None], (n, HEAD_DIM))

    head_gain = jnp.concatenate([
        rep(q_norm_na[0], nh_na), rep(k_norm_na[0], nh_na), rep(ones, nh_na),
        rep(q_norm_swa[0], hq_sw), rep(k_norm_swa[0], hkv_sw), rep(ones, hkv_sw)])[:, None, :]
    flag = np.concatenate([np.ones(2 * nh_na), np.zeros(nh_na), np.ones(hq_sw + hkv_sw),
                           np.zeros(hkv_sw)]).astype(np.float32)
    head_flag = jnp.asarray(np.broadcast_to(flag[:, None, None], (flag.shape[0], 1, HEAD_DIM)))

    w_in_b = w_in[0].astype(BF16)
    w_out_b = w_out[0].astype(BF16)
    w_up_f = w_up[0].astype(F32)
    w_down_f = w_down[0].astype(F32)
    g_attn = norm_attn[0].astype(F32)[None]
    g_mlp = norm_mlp[0].astype(F32)[None]

    mproj = _inproj(meta_tokens.astype(F32), g_attn, w_in_b, head_gain, head_flag, N_META, 512)
    k0 = 3 * nh_na + hq_sw
    mpad = jnp.pad(mproj[k0:k0 + 2 * hkv_sw], ((0, 0), (0, LANES - N_META), (0, 0)))
    meta_kv = (mproj[nh_na:2 * nh_na], mproj[2 * nh_na:3 * nh_na], mpad[:hkv_sw], mpad[hkv_sw:])

    specs, plans = _na_plan()
    tabs = (_na_tiles(na_rpb[0], specs), plans, _swa_tables(t5_bias, swa_sink[0]))
    params = (g_attn, w_in_b, head_gain, head_flag, w_out_b, g_mlp, w_up_f, w_down_f)
    y_prompt = _trunk(x_prompt, meta_kv, tabs, params)
    y_sample = _trunk(x_sample, meta_kv, tabs, params)
    return (y_prompt, y_sample)
```

```python
import functools
import math

import jax
import jax.numpy as jnp
import numpy as np
from jax import lax
from jax.experimental import pallas as pl
from jax.experimental.pallas import tpu as pltpu

F32 = jnp.float32
BF16 = jnp.bfloat16

HEAD_DIM = 128
N_META = 16
GRID_W = 64
NA_ROWS = 8
NA_COLS = 16
SWA_WINDOW = 128
SWA_BLOCK = 128
T5_BUCKETS = 32
T5_MAX_DIST = 128
NORM_EPS = 1e-6
NEG_INF = -1e30
LOG2_E = math.log2(math.e)

LANES = 128
NA_QROWS = 4
NA_QBLK = NA_QROWS * GRID_W
NA_KBLKS = 3
NA_HEADS_PER_STEP = 8
SWA_KV_HEADS_PER_STEP = 4
NA_TILES_PER_ROW = NA_KBLKS * NA_QBLK // LANES
MIB = 1024 * 1024

assert 2 * GRID_W == LANES


def _cparams(sem, vmem_mib):
    return pltpu.CompilerParams(dimension_semantics=sem, vmem_limit_bytes=vmem_mib * MIB)


def _rms_rows(x, gain):
    ms = jnp.mean(x * x, axis=-1, keepdims=True)
    return x * lax.rsqrt(ms + NORM_EPS) * gain


def _inproj_kernel(x_ref, g_ref, w_ref, hg_ref, hf_ref, o_ref, xn_ref):
    @pl.when(pl.program_id(1) == 0)
    def _():
        xn_ref[...] = _rms_rows(x_ref[...], g_ref[...]).astype(BF16)

    res = jnp.dot(xn_ref[...], w_ref[...], preferred_element_type=F32)
    for c in range(o_ref.shape[0]):
        blk = res[:, c * LANES:(c + 1) * LANES]
        r = lax.rsqrt(jnp.mean(blk * blk, axis=-1, keepdims=True) + NORM_EPS)
        f = hf_ref[c]
        o_ref[c] = (blk * (f * r + (1.0 - f)) * hg_ref[c]).astype(BF16)


def _inproj(x2, gain, w, head_gain, head_flag, bm, bn):
    m, d = x2.shape
    n = w.shape[1]
    cpb = bn // LANES
    return pl.pallas_call(
        _inproj_kernel,
        grid=(m // bm, n // bn),
        in_specs=[
            pl.BlockSpec((bm, d), lambda i, j: (i, 0)),
            pl.BlockSpec((1, d), lambda i, j: (0, 0)),
            pl.BlockSpec((d, bn), lambda i, j: (0, j)),
            pl.BlockSpec((cpb, 1, LANES), lambda i, j: (j, 0, 0)),
            pl.BlockSpec((cpb, 1, LANES), lambda i, j: (j, 0, 0)),
        ],
        out_specs=pl.BlockSpec((cpb, bm, LANES), lambda i, j: (j, i, 0)),
        out_shape=jax.ShapeDtypeStruct((n // LANES, m, LANES), BF16),
        scratch_shapes=[pltpu.VMEM((bm, d), BF16)],
        compiler_params=_cparams(("parallel", "arbitrary"), 56),
        name="inproj",
    )(x2, gain, w, head_gain, head_flag)


def _mlp_up_kernel(a_ref, ss_ref, w_ref, o_ref, *, d):
    ss = ss_ref[0][:, :1]
    for c in range(1, ss_ref.shape[0]):
        ss = ss + ss_ref[c][:, :1]
    r = lax.rsqrt(ss * (1.0 / d) + NORM_EPS)
    u = jnp.dot(a_ref[...], w_ref[...], preferred_element_type=F32) * r
    u = jnp.maximum(u, 0.0)
    o_ref[...] = (u * u).astype(BF16)


def _mlp_up(a, ss, w, bm, bn):
    m, d = a.shape
    n = w.shape[1]
    ncol = ss.shape[0]
    return pl.pallas_call(
        functools.partial(_mlp_up_kernel, d=d),
        grid=(m // bm, n // bn),
        in_specs=[
            pl.BlockSpec((bm, d), lambda i, j: (i, 0), pipeline_mode=pl.Buffered(1)),
            pl.BlockSpec((ncol, bm, LANES), lambda i, j: (0, i, 0)),
            pl.BlockSpec((d, bn), lambda i, j: (0, j)),
        ],
        out_specs=pl.BlockSpec((bm, bn), lambda i, j: (i, j)),
        out_shape=jax.ShapeDtypeStruct((m, n), BF16),
        compiler_params=_cparams(("parallel", "arbitrary"), 56),
        name="mlp_up",
    )(a, ss, w)


def _mlp_down_kernel(a_ref, w_ref, r_ref, o_ref):
    @pl.when(pl.program_id(2) == 0)
    def _():
        o_ref[...] = r_ref[...]

    o_ref[...] += jnp.dot(a_ref[...], w_ref[...], preferred_element_type=F32)


def _mlp_down(a, w, resid, bm, bn, bk):
    m, kd = a.shape
    n = w.shape[1]
    return pl.pallas_call(
        _mlp_down_kernel,
        grid=(m // bm, n // bn, kd // bk),
        in_specs=[
            pl.BlockSpec((bm, bk), lambda i, j, k: (i, k)),
            pl.BlockSpec((bk, bn), lambda i, j, k: (k, j)),
            pl.BlockSpec((bm, bn), lambda i, j, k: (i, j)),
        ],
        out_specs=pl.BlockSpec((bm, bn), lambda i, j, k: (i, j)),
        out_shape=jax.ShapeDtypeStruct((m, n), F32),
        compiler_params=_cparams(("parallel", "parallel", "arbitrary"), 60),
        name="mlp_down",
    )(a, w, resid)


def _outproj_kernel(na_ref, sw_ref, wt_ref, wb_ref, r_ref, g_ref, o_ref, ob_ref, ss_ref):
    acc = jnp.dot(na_ref[...], wt_ref[...], preferred_element_type=F32)
    acc = acc + jnp.dot(sw_ref[...], wb_ref[...], preferred_element_type=F32)
    x1 = r_ref[...] + acc
    o_ref[...] = x1
    ob_ref[...] = (x1 * g_ref[...]).astype(BF16)
    ss_ref[...] = jnp.broadcast_to(jnp.sum(x1 * x1, axis=-1, keepdims=True), ss_ref.shape)


def _outproj(na_o, sw_o, w, resid, gain, bm, bn):
    m, kh = na_o.shape
    n = w.shape[1]
    return pl.pallas_call(
        _outproj_kernel,
        grid=(n // bn, m // bm),
        in_specs=[
            pl.BlockSpec((bm, kh), lambda j, i: (i, 0)),
            pl.BlockSpec((bm, kh), lambda j, i: (i, 0)),
            pl.BlockSpec((kh, bn), lambda j, i: (0, j), pipeline_mode=pl.Buffered(1)),
            pl.BlockSpec((kh, bn), lambda j, i: (1, j), pipeline_mode=pl.Buffered(1)),
            pl.BlockSpec((bm, bn), lambda j, i: (i, j)),
            pl.BlockSpec((1, bn), lambda j, i: (0, j)),
        ],
        out_specs=[pl.BlockSpec((bm, bn), lambda j, i: (i, j)),
                   pl.BlockSpec((bm, bn), lambda j, i: (i, j)),
                   pl.BlockSpec((None, bm, LANES), lambda j, i: (j, i, 0))],
        out_shape=[jax.ShapeDtypeStruct((m, n), F32),
                   jax.ShapeDtypeStruct((m, n), BF16),
                   jax.ShapeDtypeStruct((n // bn, m, LANES), F32)],
        compiler_params=_cparams(("parallel", "parallel"), 56),
        name="outproj",
    )(na_o, sw_o, w, w, resid, gain)


def _qkt(q, k):
    return lax.dot_general(q, k, (((1,), (1,)), ((), ())), preferred_element_type=F32)


def _na_plan():
    vblk = 6
    vrows = vblk * NA_QROWS
    wrows = NA_KBLKS * NA_QROWS
    specs, plans = [], []
    for j in (0, 2, vblk - 1):
        ws = int(np.clip(j - 1, 0, vblk - NA_KBLKS))
        meta_row = 0 if j == vblk - 1 else wrows - 1
        plan = []
        for qrl in range(NA_QROWS):
            qr = NA_QROWS * j + qrl
            rs = int(np.clip(qr - NA_ROWS // 2, 0, vrows - NA_ROWS))
            row = []
            for t in range(NA_TILES_PER_ROW):
                halves = []
                for krl in (2 * t, 2 * t + 1):
                    kr = NA_QROWS * ws + krl
                    half = kr - qr if rs <= kr < rs + NA_ROWS else None
                    if krl == meta_row:
                        assert half is None
                        half = "meta"
                    halves.append(half)
                halves = tuple(halves)
                if halves == (None, None):
                    row.append(-1)
                else:
                    if halves not in specs:
                        specs.append(halves)
                    row.append(specs.index(halves))
            plan.append(row)
        plans.append(plan)
    return specs, plans


def _na_tiles(rpb, specs):
    nh = rpb.shape[0]
    c = np.arange(GRID_W)
    cs = np.clip(c - NA_COLS // 2, 0, GRID_W - NA_COLS)
    col_in = (c[None, :] >= cs[:, None]) & (c[None, :] < cs[:, None] + NA_COLS)
    dc = np.clip(c[None, :] - c[:, None], -(NA_COLS - 1), NA_COLS - 1) + (NA_COLS - 1)
    onehot = np.zeros((2 * NA_COLS - 1, GRID_W * GRID_W), np.float32)
    onehot[dc.reshape(-1), np.arange(GRID_W * GRID_W)] = 1.0
    tcol = jnp.einsum("hab,bx->hax", rpb.astype(F32), onehot, precision=lax.Precision.HIGHEST)
    tcol = jnp.where(col_in[None, None], tcol.reshape(nh, 2 * NA_ROWS - 1, GRID_W, GRID_W), NEG_INF)
    masked = jnp.full((nh, GRID_W, GRID_W), NEG_INF, F32)
    meta = jnp.where(np.arange(GRID_W) < N_META, 0.0, masked)

    def half(dr):
        if dr is None:
            return masked
        return meta if dr == "meta" else tcol[:, dr + NA_ROWS - 1]

    tiles = jnp.stack([jnp.concatenate([half(l), half(r)], axis=-1) for l, r in specs], axis=1)
    return tiles * LOG2_E


def _na_kernel(q_ref, k0_ref, k1_ref, k2_ref, v0_ref, v1_ref, v2_ref, km_ref, vm_ref, tiles_ref,
               o_ref, bias_ref, k_scr, v_scr, *, scale, plans):
    hb = q_ref.shape[0]
    j = pl.program_id(2)
    last = pl.num_programs(2) - 1
    masked_tile = jnp.full((GRID_W, LANES), NEG_INF, F32)

    def build(plan):
        def body():
            for hh in range(hb):
                for qrl in range(NA_QROWS):
                    for t in range(NA_TILES_PER_ROW):
                        idx = plan[qrl][t]
                        tile = masked_tile if idx < 0 else tiles_ref[hh, idx]
                        bias_ref[hh, qrl * GRID_W:(qrl + 1) * GRID_W, t * LANES:(t + 1) * LANES] = tile
        return body

    pl.when(j == 0)(build(plans[0]))
    pl.when(j == 1)(build(plans[1]))
    pl.when(j == last)(build(plans[2]))

    meta_pos = pl.multiple_of(
        jnp.where(j == last, 0, (NA_KBLKS * NA_QROWS - 1) * GRID_W), N_META)
    for hh in range(hb):
        for d, (kb_ref, vb_ref) in enumerate(((k0_ref, v0_ref), (k1_ref, v1_ref), (k2_ref, v2_ref))):
            k_scr[hh, d * NA_QBLK:(d + 1) * NA_QBLK, :] = kb_ref[hh]
            v_scr[hh, d * NA_QBLK:(d + 1) * NA_QBLK, :] = vb_ref[hh]
        k_scr[hh, pl.ds(meta_pos, N_META), :] = km_ref[hh]
        v_scr[hh, pl.ds(meta_pos, N_META), :] = vm_ref[hh]
        q = q_ref[hh]
        k = k_scr[hh]
        v = v_scr[hh]
        s = _qkt(q, k) * (scale * LOG2_E) + bias_ref[hh]
        p = jnp.exp2(s - jnp.max(s, axis=-1, keepdims=True))
        den = jnp.sum(p, axis=-1, keepdims=True)
        o = jnp.dot(p.astype(BF16), v, preferred_element_type=F32)
        o_ref[:, hh * LANES:(hh + 1) * LANES] = (o / den).astype(BF16)


def _na_attention(proj, km, vm, tiles, plans, batch, rows):
    nh = km.shape[0]
    hb = NA_HEADS_PER_STEP
    nblk = rows // NA_QROWS
    tokens = proj.shape[1]
    assert nblk >= NA_KBLKS and nh % hb == 0
    hblks = nh // hb

    def kv_map(base, d):
        def f(h, b, j):
            ws = jnp.clip(j - 1, 0, nblk - NA_KBLKS)
            return (base + h, b * nblk + ws + d, 0)
        return f

    qspec = pl.BlockSpec((hb, NA_QBLK, LANES), lambda h, b, j: (h, b * nblk + j, 0))
    kspecs = [pl.BlockSpec((hb, NA_QBLK, LANES), kv_map(hblks, d)) for d in range(NA_KBLKS)]
    vspecs = [pl.BlockSpec((hb, NA_QBLK, LANES), kv_map(2 * hblks, d)) for d in range(NA_KBLKS)]
    mspec = pl.BlockSpec((hb, N_META, LANES), lambda h, b, j: (h, 0, 0))
    tspec = pl.BlockSpec((hb,) + tiles.shape[1:], lambda h, b, j: (h, 0, 0, 0))
    return pl.pallas_call(
        functools.partial(_na_kernel, scale=HEAD_DIM ** -0.5, plans=plans),
        grid=(hblks, batch, nblk),
        in_specs=[qspec] + kspecs + vspecs + [mspec, mspec, tspec],
        out_specs=pl.BlockSpec((NA_QBLK, hb * LANES), lambda h, b, j: (b * nblk + j, h)),
        out_shape=jax.ShapeDtypeStruct((tokens, nh * LANES), BF16),
        scratch_shapes=[pltpu.VMEM((hb, NA_QBLK, NA_KBLKS * NA_QBLK), F32),
                        pltpu.VMEM((hb, NA_KBLKS * NA_QBLK, LANES), BF16),
                        pltpu.VMEM((hb, NA_KBLKS * NA_QBLK, LANES), BF16)],
        compiler_params=_cparams(("arbitrary", "arbitrary", "arbitrary"), 48),
        name="na_attn",
    )(proj, proj, proj, proj, proj, proj, proj, km, vm, tiles)


def _swa_kernel(q_ref, k0_ref, k1_ref, k2_ref, v0_ref, v1_ref, v2_ref, km_ref, vm_ref, bias_ref,
                o_ref, *, scale, group):
    hkb = k0_ref.shape[0]
    blk = q_ref.shape[1]
    for kh in range(hkb):
        heads = slice(kh * group, (kh + 1) * group)
        q = q_ref[heads].reshape(group * blk, LANES)
        k = jnp.concatenate([k0_ref[kh], k1_ref[kh], k2_ref[kh], km_ref[kh]], axis=0)
        v = jnp.concatenate([v0_ref[kh], v1_ref[kh], v2_ref[kh], vm_ref[kh]], axis=0)
        s = _qkt(q, k) * (scale * LOG2_E) + bias_ref[heads].reshape(group * blk, 3 * blk + LANES)
        p = jnp.exp2(s - jnp.max(s, axis=-1, keepdims=True))
        den = jnp.sum(p, axis=-1, keepdims=True)
        o = (jnp.dot(p.astype(BF16), v, preferred_element_type=F32) / den).astype(BF16)
        for g in range(group):
            col = (kh * group + g) * LANES
            o_ref[:, col:col + LANES] = o[g * blk:(g + 1) * blk]


def _swa_attention(proj, km, vm, bias, batch, seq, q_base, k_base, v_base):
    hkv = km.shape[0]
    hq = bias.shape[1]
    group = hq // hkv
    hkb = SWA_KV_HEADS_PER_STEP
    blk = SWA_BLOCK
    nb = seq // blk
    tokens = proj.shape[1]
    assert hkv % hkb == 0 and k_base % hkb == 0 and v_base % hkb == 0 and q_base % (hkb * group) == 0

    def bias_map(hk, b, i):
        variant = jnp.where(i == 0, 1, 0) + jnp.where(i == nb - 1, 2, 0)
        return (variant, hk, 0, 0)

    def kv_map(base, d):
        def f(hk, b, i):
            return (base // hkb + hk, b * nb + jnp.clip(i - 1 + d, 0, nb - 1), 0)
        return f

    kspecs = [pl.BlockSpec((hkb, blk, LANES), kv_map(k_base, d)) for d in range(3)]
    vspecs = [pl.BlockSpec((hkb, blk, LANES), kv_map(v_base, d)) for d in range(3)]
    mspec = pl.BlockSpec((hkb, LANES, LANES), lambda hk, b, i: (hk, 0, 0))
    return pl.pallas_call(
        functools.partial(_swa_kernel, scale=HEAD_DIM ** -0.5, group=group),
        grid=(hkv // hkb, batch, nb),
        in_specs=[pl.BlockSpec((hkb * group, blk, LANES),
                               lambda hk, b, i: (q_base // (hkb * group) + hk, b * nb + i, 0))]
                 + kspecs + vspecs + [mspec, mspec,
                  pl.BlockSpec((None, hkb * group, blk, 3 * blk + LANES), bias_map)],
        out_specs=pl.BlockSpec((blk, hkb * group * LANES), lambda hk, b, i: (b * nb + i, hk)),
        out_shape=jax.ShapeDtypeStruct((tokens, hq * LANES), BF16),
        compiler_params=_cparams(("parallel", "parallel", "arbitrary"), 32),
        name="swa_attn",
    )(proj, proj, proj, proj, proj, proj, proj, km, vm, bias)


def _t5_bucket(rel):
    nb = T5_BUCKETS // 2
    max_exact = nb // 2
    ret = np.where(rel > 0, nb, 0)
    n = np.abs(rel)
    large = max_exact + (np.log(np.maximum(n, 1) / max_exact)
                         / math.log(T5_MAX_DIST / max_exact) * (nb - max_exact)).astype(np.int64)
    large = np.minimum(large, nb - 1)
    return ret + np.where(n < max_exact, n, large)


def _select_rows(table, idx):
    flat = idx.reshape(-1)
    onehot = np.zeros((table.shape[0], flat.size), np.float32)
    onehot[flat, np.arange(flat.size)] = 1.0
    out = jnp.einsum("bh,bx->xh", table.astype(F32), onehot, precision=lax.Precision.HIGHEST)
    return out.reshape(idx.shape + (table.shape[1],))


def _swa_tables(t5_bias, sink):
    blk = SWA_BLOCK
    rel = np.arange(-(2 * blk - 1), 2 * blk)
    by_rel = _select_rows(t5_bias, _t5_bucket(rel)).T
    by_rel = jnp.where((np.abs(rel) <= SWA_WINDOW)[None], by_rel, NEG_INF)
    bias_w = jnp.stack([by_rel[:, blk - 1 - q:blk - 1 - q + 3 * blk] for q in range(blk)], axis=1)
    jj = np.arange(3 * blk)
    edge = np.stack([np.zeros_like(jj, bool), jj < blk, jj >= 2 * blk, (jj < blk) | (jj >= 2 * blk)])
    bias_w = jnp.where(edge[:, None, None, :], NEG_INF, bias_w[None])
    assert N_META + blk - (N_META - 1) > T5_MAX_DIST
    rel_m = np.arange(N_META)[None, :] - (N_META + np.arange(2 * blk))[:, None]
    bias_m = _select_rows(t5_bias, _t5_bucket(rel_m))
    bias_m = bias_m.reshape(2, blk, N_META, -1).transpose(0, 3, 1, 2)
    bias_m = jnp.stack([bias_m[1], bias_m[0], bias_m[1], bias_m[0]])
    hq = bias_m.shape[1]
    sink_col = jnp.broadcast_to(sink.astype(F32)[None, :, None, None], (4, hq, blk, 1))
    pad = jnp.full((4, hq, blk, LANES - N_META - 1), NEG_INF, F32)
    return jnp.concatenate([bias_w, bias_m, sink_col, pad], axis=-1) * LOG2_E


def _trunk(x, meta_kv, tabs, params):
    batch, seq, d = x.shape
    (g_attn, w_in, head_gain, head_flag, w_out, g_mlp, w_up, w_down) = params
    km_na, vm_na, km_sw, vm_sw = meta_kv
    na_tiles, na_plans, swa_bias = tabs
    nh_na = km_na.shape[0]
    hq_sw = swa_bias.shape[1]
    hkv_sw = km_sw.shape[0]
    x2 = x.reshape(batch * seq, d)
    proj = _inproj(x2, g_attn, w_in, head_gain, head_flag, 512, 1024)
    rows = seq // GRID_W
    assert seq % (GRID_W * NA_QROWS) == 0 and rows >= NA_ROWS
    assert seq % SWA_BLOCK == 0
    na_o = _na_attention(proj, km_na, vm_na, na_tiles, na_plans, batch, rows)
    q_base = 3 * nh_na
    k_base = q_base + hq_sw
    v_base = k_base + hkv_sw
    sw_o = _swa_attention(proj, km_sw, vm_sw, swa_bias, batch, seq, q_base, k_base, v_base)
    x1, x1g, x1ss = _outproj(na_o, sw_o, w_out, x2, g_mlp, 512, 2048)
    u = _mlp_up(x1g, x1ss, w_up, 1024, 1024)
    y = _mlp_down(u, w_down, x1, 1024, 1024, 4096)
    return y.reshape(batch, seq, d)


def kernel(x_prompt, x_sample, meta_tokens, t5_bias, norm_attn, w_in, q_norm_na, k_norm_na, na_rpb,
           q_norm_swa, k_norm_swa, swa_sink, w_out, norm_mlp, w_up, w_down):
    depth = w_in.shape[0]
    assert depth == 1
    nh_na = na_rpb.shape[1]
    hq_sw = swa_sink.shape[1]
    in_width = w_in.shape[2]
    hkv_sw = (in_width // HEAD_DIM - 3 * nh_na - hq_sw) // 2

    ones = jnp.ones((HEAD_DIM,), F32)

    def rep(g, n):
        return jnp.broadcast_to(g.astype(F32)[None], (n, HEAD_DIM))

    head_gain = jnp.concatenate([
        rep(q_norm_na[0], nh_na), rep(k_norm_na[0], nh_na), rep(ones, nh_na),
        rep(q_norm_swa[0], hq_sw), rep(k_norm_swa[0], hkv_sw), rep(ones, hkv_sw)])[:, None, :]
    flag = np.concatenate([np.ones(2 * nh_na), np.zeros(nh_na), np.ones(hq_sw + hkv_sw),
                           np.zeros(hkv_sw)]).astype(np.float32)
    head_flag = jnp.asarray(np.broadcast_to(flag[:, None, None], (flag.shape[0], 1, HEAD_DIM)))

    w_in_b = w_in[0].astype(BF16)
    w_out_b = w_out[0].astype(BF16)
    w_up_f = w_up[0].astype(F32)
    w_down_b = w_down[0].astype(BF16)
    g_attn = norm_attn[0].astype(F32)[None]
    g_mlp = norm_mlp[0].astype(F32)[None]

    mproj = _inproj(meta_tokens.astype(F32), g_attn, w_in_b, head_gain, head_flag, N_META, 512)
    k0 = 3 * nh_na + hq_sw
    mpad = jnp.pad(mproj[k0:k0 + 2 * hkv_sw], ((0, 0), (0, LANES - N_META), (0, 0)))
    meta_kv = (mproj[nh_na:2 * nh_na], mproj[2 * nh_na:3 * nh_na], mpad[:hkv_sw], mpad[hkv_sw:])

    specs, plans = _na_plan()
    tabs = (_na_tiles(na_rpb[0], specs), plans, _swa_tables(t5_bias, swa_sink[0]))
    params = (g_attn, w_in_b, head_gain, head_flag, w_out_b, g_mlp, w_up_f, w_down_b)
    y_prompt = _trunk(x_prompt, meta_kv, tabs, params)
    y_sample = _trunk(x_sample, meta_kv, tabs, params)
    return (y_prompt, y_sample)
```

```python
import functools
import math

import jax
import jax.numpy as jnp
import numpy as np
from jax import lax
from jax.experimental import pallas as pl
from jax.experimental.pallas import tpu as pltpu

F32 = jnp.float32
BF16 = jnp.bfloat16

HEAD_DIM = 128
N_META = 16
GRID_W = 64
NA_ROWS = 8
NA_COLS = 16
SWA_WINDOW = 128
SWA_BLOCK = 128
T5_BUCKETS = 32
T5_MAX_DIST = 128
NORM_EPS = 1e-6
NEG_INF = -1e30
LOG2_E = math.log2(math.e)

LANES = 128
NA_QROWS = 4
NA_QBLK = NA_QROWS * GRID_W
NA_KBLKS = 3
NA_HEADS_PER_STEP = 8
SWA_KV_HEADS_PER_STEP = 4
NA_TILES_PER_ROW = NA_KBLKS * NA_QBLK // LANES
MIB = 1024 * 1024

assert 2 * GRID_W == LANES


def _cparams(sem, vmem_mib):
    return pltpu.CompilerParams(dimension_semantics=sem, vmem_limit_bytes=vmem_mib * MIB)


def _rms_rows(x, gain):
    ms = jnp.mean(x * x, axis=-1, keepdims=True)
    return x * lax.rsqrt(ms + NORM_EPS) * gain


def _inproj_kernel(x_ref, g_ref, w_ref, hg_ref, hf_ref, o_ref, xn_ref):
    @pl.when(pl.program_id(1) == 0)
    def _():
        xn_ref[...] = _rms_rows(x_ref[...], g_ref[...]).astype(BF16)

    res = jnp.dot(xn_ref[...], w_ref[...], preferred_element_type=F32)
    for c in range(o_ref.shape[0]):
        blk = res[:, c * LANES:(c + 1) * LANES]
        r = lax.rsqrt(jnp.mean(blk * blk, axis=-1, keepdims=True) + NORM_EPS)
        f = hf_ref[c]
        o_ref[c] = (blk * (f * r + (1.0 - f)) * hg_ref[c]).astype(BF16)


def _inproj(x2, gain, w, head_gain, head_flag, bm, bn):
    m, d = x2.shape
    n = w.shape[1]
    cpb = bn // LANES
    return pl.pallas_call(
        _inproj_kernel,
        grid=(m // bm, n // bn),
        in_specs=[
            pl.BlockSpec((bm, d), lambda i, j: (i, 0)),
            pl.BlockSpec((1, d), lambda i, j: (0, 0)),
            pl.BlockSpec((d, bn), lambda i, j: (0, j)),
            pl.BlockSpec((cpb, 1, LANES), lambda i, j: (j, 0, 0)),
            pl.BlockSpec((cpb, 1, LANES), lambda i, j: (j, 0, 0)),
        ],
        out_specs=pl.BlockSpec((cpb, bm, LANES), lambda i, j: (j, i, 0)),
        out_shape=jax.ShapeDtypeStruct((n // LANES, m, LANES), BF16),
        scratch_shapes=[pltpu.VMEM((bm, d), BF16)],
        compiler_params=_cparams(("parallel", "arbitrary"), 56),
        name="inproj",
    )(x2, gain, w, head_gain, head_flag)


def _mlp_up_kernel(a_ref, ss_ref, w_ref, o_ref, *, d):
    ss = ss_ref[0][:, :1]
    for c in range(1, ss_ref.shape[0]):
        ss = ss + ss_ref[c][:, :1]
    r = lax.rsqrt(ss * (1.0 / d) + NORM_EPS)
    u = jnp.dot(a_ref[...], w_ref[...], preferred_element_type=F32) * r
    u = jnp.maximum(u, 0.0)
    o_ref[...] = (u * u).astype(BF16)


def _mlp_up(a, ss, w, bm, bn):
    m, d = a.shape
    n = w.shape[1]
    ncol = ss.shape[0]
    return pl.pallas_call(
        functools.partial(_mlp_up_kernel, d=d),
        grid=(m // bm, n // bn),
        in_specs=[
            pl.BlockSpec((bm, d), lambda i, j: (i, 0), pipeline_mode=pl.Buffered(1)),
            pl.BlockSpec((ncol, bm, LANES), lambda i, j: (0, i, 0)),
            pl.BlockSpec((d, bn), lambda i, j: (0, j)),
        ],
        out_specs=pl.BlockSpec((bm, bn), lambda i, j: (i, j)),
        out_shape=jax.ShapeDtypeStruct((m, n), BF16),
        compiler_params=_cparams(("parallel", "arbitrary"), 56),
        name="mlp_up",
    )(a, ss, w)


def _mlp_down_kernel(a_ref, w_ref, r_ref, o_ref):
    @pl.when(pl.program_id(2) == 0)
    def _():
        o_ref[...] = r_ref[...]

    o_ref[...] += jnp.dot(a_ref[...], w_ref[...], preferred_element_type=F32)


def _mlp_down(a, w, resid, bm, bn, bk):
    m, kd = a.shape
    n = w.shape[1]
    return pl.pallas_call(
        _mlp_down_kernel,
        grid=(m // bm, n // bn, kd // bk),
        in_specs=[
            pl.BlockSpec((bm, bk), lambda i, j, k: (i, k)),
            pl.BlockSpec((bk, bn), lambda i, j, k: (k, j)),
            pl.BlockSpec((bm, bn), lambda i, j, k: (i, j)),
        ],
        out_specs=pl.BlockSpec((bm, bn), lambda i, j, k: (i, j)),
        out_shape=jax.ShapeDtypeStruct((m, n), F32),
        compiler_params=_cparams(("parallel", "parallel", "arbitrary"), 60),
        name="mlp_down",
    )(a, w, resid)


def _outproj_kernel(na_ref, sw_ref, wt_ref, wb_ref, r_ref, g_ref, o_ref, ob_ref, ss_ref):
    acc = jnp.dot(na_ref[...], wt_ref[...], preferred_element_type=F32)
    acc = acc + jnp.dot(sw_ref[...], wb_ref[...], preferred_element_type=F32)
    x1 = r_ref[...] + acc
    o_ref[...] = x1
    ob_ref[...] = (x1 * g_ref[...]).astype(BF16)
    ss_ref[...] = jnp.broadcast_to(jnp.sum(x1 * x1, axis=-1, keepdims=True), ss_ref.shape)


def _outproj(na_o, sw_o, w, resid, gain, bm, bn):
    m, kh = na_o.shape
    n = w.shape[1]
    return pl.pallas_call(
        _outproj_kernel,
        grid=(n // bn, m // bm),
        in_specs=[
            pl.BlockSpec((bm, kh), lambda j, i: (i, 0)),
            pl.BlockSpec((bm, kh), lambda j, i: (i, 0)),
            pl.BlockSpec((kh, bn), lambda j, i: (0, j), pipeline_mode=pl.Buffered(1)),
            pl.BlockSpec((kh, bn), lambda j, i: (1, j), pipeline_mode=pl.Buffered(1)),
            pl.BlockSpec((bm, bn), lambda j, i: (i, j)),
            pl.BlockSpec((1, bn), lambda j, i: (0, j)),
        ],
        out_specs=[pl.BlockSpec((bm, bn), lambda j, i: (i, j)),
                   pl.BlockSpec((bm, bn), lambda j, i: (i, j)),
                   pl.BlockSpec((None, bm, LANES), lambda j, i: (j, i, 0))],
        out_shape=[jax.ShapeDtypeStruct((m, n), F32),
                   jax.ShapeDtypeStruct((m, n), BF16),
                   jax.ShapeDtypeStruct((n // bn, m, LANES), F32)],
        compiler_params=_cparams(("parallel", "parallel"), 56),
        name="outproj",
    )(na_o, sw_o, w, w, resid, gain)


def _qkt(q, k):
    return lax.dot_general(q, k, (((1,), (1,)), ((), ())), preferred_element_type=F32)


def _softmax_pv(s, v):
    p = jnp.exp2(s - jnp.max(s, axis=-1, keepdims=True)).astype(BF16)
    o = jnp.dot(p, jnp.concatenate([v, jnp.ones_like(v)], axis=1), preferred_element_type=F32)
    return (o[:, :LANES] / o[:, LANES:]).astype(BF16)


def _na_plan():
    vblk = 6
    vrows = vblk * NA_QROWS
    wrows = NA_KBLKS * NA_QROWS
    specs, plans = [], []
    for j in (0, 2, vblk - 1):
        ws = int(np.clip(j - 1, 0, vblk - NA_KBLKS))
        meta_row = 0 if j == vblk - 1 else wrows - 1
        plan = []
        for qrl in range(NA_QROWS):
            qr = NA_QROWS * j + qrl
            rs = int(np.clip(qr - NA_ROWS // 2, 0, vrows - NA_ROWS))
            row = []
            for t in range(NA_TILES_PER_ROW):
                halves = []
                for krl in (2 * t, 2 * t + 1):
                    kr = NA_QROWS * ws + krl
                    half = kr - qr if rs <= kr < rs + NA_ROWS else None
                    if krl == meta_row:
                        assert half is None
                        half = "meta"
                    halves.append(half)
                halves = tuple(halves)
                if halves == (None, None):
                    row.append(-1)
                else:
                    if halves not in specs:
                        specs.append(halves)
                    row.append(specs.index(halves))
            plan.append(row)
        plans.append(plan)
    return specs, plans


def _na_tiles(rpb, specs):
    nh = rpb.shape[0]
    c = np.arange(GRID_W)
    cs = np.clip(c - NA_COLS // 2, 0, GRID_W - NA_COLS)
    col_in = (c[None, :] >= cs[:, None]) & (c[None, :] < cs[:, None] + NA_COLS)
    dc = np.clip(c[None, :] - c[:, None], -(NA_COLS - 1), NA_COLS - 1) + (NA_COLS - 1)
    onehot = np.zeros((2 * NA_COLS - 1, GRID_W * GRID_W), np.float32)
    onehot[dc.reshape(-1), np.arange(GRID_W * GRID_W)] = 1.0
    tcol = jnp.einsum("hab,bx->hax", rpb.astype(F32), onehot, precision=lax.Precision.HIGHEST)
    tcol = jnp.where(col_in[None, None], tcol.reshape(nh, 2 * NA_ROWS - 1, GRID_W, GRID_W), NEG_INF)
    masked = jnp.full((nh, GRID_W, GRID_W), NEG_INF, F32)
    meta = jnp.where(np.arange(GRID_W) < N_META, 0.0, masked)

    def half(dr):
        if dr is None:
            return masked
        return meta if dr == "meta" else tcol[:, dr + NA_ROWS - 1]

    tiles = jnp.stack([jnp.concatenate([half(l), half(r)], axis=-1) for l, r in specs], axis=1)
    return tiles * LOG2_E


def _na_kernel(q_ref, k0_ref, k1_ref, k2_ref, v0_ref, v1_ref, v2_ref, km_ref, vm_ref, tiles_ref,
               o_ref, bias_ref, k_scr, v_scr, *, scale, plans):
    hb = q_ref.shape[0]
    j = pl.program_id(2)
    last = pl.num_programs(2) - 1
    masked_tile = jnp.full((GRID_W, LANES), NEG_INF, F32)

    def build(plan):
        def body():
            for hh in range(hb):
                for qrl in range(NA_QROWS):
                    for t in range(NA_TILES_PER_ROW):
                        idx = plan[qrl][t]
                        tile = masked_tile if idx < 0 else tiles_ref[hh, idx]
                        bias_ref[hh, qrl * GRID_W:(qrl + 1) * GRID_W, t * LANES:(t + 1) * LANES] = tile
        return body

    pl.when(j == 0)(build(plans[0]))
    pl.when(j == 1)(build(plans[1]))
    pl.when(j == last)(build(plans[2]))

    meta_pos = pl.multiple_of(
        jnp.where(j == last, 0, (NA_KBLKS * NA_QROWS - 1) * GRID_W), N_META)
    for hh in range(hb):
        for d, (kb_ref, vb_ref) in enumerate(((k0_ref, v0_ref), (k1_ref, v1_ref), (k2_ref, v2_ref))):
            k_scr[hh, d * NA_QBLK:(d + 1) * NA_QBLK, :] = kb_ref[hh]
            v_scr[hh, d * NA_QBLK:(d + 1) * NA_QBLK, :] = vb_ref[hh]
        k_scr[hh, pl.ds(meta_pos, N_META), :] = km_ref[hh]
        v_scr[hh, pl.ds(meta_pos, N_META), :] = vm_ref[hh]
        q = q_ref[hh]
        k = k_scr[hh]
        v = v_scr[hh]
        s = _qkt(q, k) * (scale * LOG2_E) + bias_ref[hh]
        o_ref[:, hh * LANES:(hh + 1) * LANES] = _softmax_pv(s, v)


def _na_attention(proj, km, vm, tiles, plans, batch, rows):
    nh = km.shape[0]
    hb = NA_HEADS_PER_STEP
    nblk = rows // NA_QROWS
    tokens = proj.shape[1]
    assert nblk >= NA_KBLKS and nh % hb == 0
    hblks = nh // hb

    def kv_map(base, d):
        def f(h, b, j):
            ws = jnp.clip(j - 1, 0, nblk - NA_KBLKS)
            return (base + h, b * nblk + ws + d, 0)
        return f

    qspec = pl.BlockSpec((hb, NA_QBLK, LANES), lambda h, b, j: (h, b * nblk + j, 0))
    kspecs = [pl.BlockSpec((hb, NA_QBLK, LANES), kv_map(hblks, d)) for d in range(NA_KBLKS)]
    vspecs = [pl.BlockSpec((hb, NA_QBLK, LANES), kv_map(2 * hblks, d)) for d in range(NA_KBLKS)]
    mspec = pl.BlockSpec((hb, N_META, LANES), lambda h, b, j: (h, 0, 0))
    tspec = pl.BlockSpec((hb,) + tiles.shape[1:], lambda h, b, j: (h, 0, 0, 0))
    return pl.pallas_call(
        functools.partial(_na_kernel, scale=HEAD_DIM ** -0.5, plans=plans),
        grid=(hblks, batch, nblk),
        in_specs=[qspec] + kspecs + vspecs + [mspec, mspec, tspec],
        out_specs=pl.BlockSpec((NA_QBLK, hb * LANES), lambda h, b, j: (b * nblk + j, h)),
        out_shape=jax.ShapeDtypeStruct((tokens, nh * LANES), BF16),
        scratch_shapes=[pltpu.VMEM((hb, NA_QBLK, NA_KBLKS * NA_QBLK), F32),
                        pltpu.VMEM((hb, NA_KBLKS * NA_QBLK, LANES), BF16),
                        pltpu.VMEM((hb, NA_KBLKS * NA_QBLK, LANES), BF16)],
        compiler_params=_cparams(("arbitrary", "arbitrary", "arbitrary"), 48),
        name="na_attn",
    )(proj, proj, proj, proj, proj, proj, proj, km, vm, tiles)


def _swa_kernel(q_ref, k0_ref, k1_ref, k2_ref, v0_ref, v1_ref, v2_ref, km_ref, vm_ref, bias_ref,
                o_ref, *, scale, group):
    hkb = k0_ref.shape[0]
    blk = q_ref.shape[1]
    for kh in range(hkb):
        heads = slice(kh * group, (kh + 1) * group)
        q = q_ref[heads].reshape(group * blk, LANES)
        k = jnp.concatenate([k0_ref[kh], k1_ref[kh], k2_ref[kh], km_ref[kh]], axis=0)
        v = jnp.concatenate([v0_ref[kh], v1_ref[kh], v2_ref[kh], vm_ref[kh]], axis=0)
        s = _qkt(q, k) * (scale * LOG2_E) + bias_ref[heads].reshape(group * blk, 3 * blk + LANES)
        o = _softmax_pv(s, v)
        for g in range(group):
            col = (kh * group + g) * LANES
            o_ref[:, col:col + LANES] = o[g * blk:(g + 1) * blk]


def _swa_attention(proj, km, vm, bias, batch, seq, q_base, k_base, v_base):
    hkv = km.shape[0]
    hq = bias.shape[1]
    group = hq // hkv
    hkb = SWA_KV_HEADS_PER_STEP
    blk = SWA_BLOCK
    nb = seq // blk
    tokens = proj.shape[1]
    assert hkv % hkb == 0 and k_base % hkb == 0 and v_base % hkb == 0 and q_base % (hkb * group) == 0

    def bias_map(hk, b, i):
        variant = jnp.where(i == 0, 1, 0) + jnp.where(i == nb - 1, 2, 0)
        return (variant, hk, 0, 0)

    def kv_map(base, d):
        def f(hk, b, i):
            return (base // hkb + hk, b * nb + jnp.clip(i - 1 + d, 0, nb - 1), 0)
        return f

    kspecs = [pl.BlockSpec((hkb, blk, LANES), kv_map(k_base, d)) for d in range(3)]
    vspecs = [pl.BlockSpec((hkb, blk, LANES), kv_map(v_base, d)) for d in range(3)]
    mspec = pl.BlockSpec((hkb, LANES, LANES), lambda hk, b, i: (hk, 0, 0))
    return pl.pallas_call(
        functools.partial(_swa_kernel, scale=HEAD_DIM ** -0.5, group=group),
        grid=(hkv // hkb, batch, nb),
        in_specs=[pl.BlockSpec((hkb * group, blk, LANES),
                               lambda hk, b, i: (q_base // (hkb * group) + hk, b * nb + i, 0))]
                 + kspecs + vspecs + [mspec, mspec,
                  pl.BlockSpec((None, hkb * group, blk, 3 * blk + LANES), bias_map)],
        out_specs=pl.BlockSpec((blk, hkb * group * LANES), lambda hk, b, i: (b * nb + i, hk)),
        out_shape=jax.ShapeDtypeStruct((tokens, hq * LANES), BF16),
        compiler_params=_cparams(("parallel", "parallel", "arbitrary"), 32),
        name="swa_attn",
    )(proj, proj, proj, proj, proj, proj, proj, km, vm, bias)


def _t5_bucket(rel):
    nb = T5_BUCKETS // 2
    max_exact = nb // 2
    ret = np.where(rel > 0, nb, 0)
    n = np.abs(rel)
    large = max_exact + (np.log(np.maximum(n, 1) / max_exact)
                         / math.log(T5_MAX_DIST / max_exact) * (nb - max_exact)).astype(np.int64)
    large = np.minimum(large, nb - 1)
    return ret + np.where(n < max_exact, n, large)


def _select_rows(table, idx):
    flat = idx.reshape(-1)
    onehot = np.zeros((table.shape[0], flat.size), np.float32)
    onehot[flat, np.arange(flat.size)] = 1.0
    out = jnp.einsum("bh,bx->xh", table.astype(F32), onehot, precision=lax.Precision.HIGHEST)
    return out.reshape(idx.shape + (table.shape[1],))


def _swa_tables(t5_bias, sink):
    blk = SWA_BLOCK
    rel = np.arange(-(2 * blk - 1), 2 * blk)
    by_rel = _select_rows(t5_bias, _t5_bucket(rel)).T
    by_rel = jnp.where((np.abs(rel) <= SWA_WINDOW)[None], by_rel, NEG_INF)
    bias_w = jnp.stack([by_rel[:, blk - 1 - q:blk - 1 - q + 3 * blk] for q in range(blk)], axis=1)
    jj = np.arange(3 * blk)
    edge = np.stack([np.zeros_like(jj, bool), jj < blk, jj >= 2 * blk, (jj < blk) | (jj >= 2 * blk)])
    bias_w = jnp.where(edge[:, None, None, :], NEG_INF, bias_w[None])
    assert N_META + blk - (N_META - 1) > T5_MAX_DIST
    rel_m = np.arange(N_META)[None, :] - (N_META + np.arange(2 * blk))[:, None]
    bias_m = _select_rows(t5_bias, _t5_bucket(rel_m))
    bias_m = bias_m.reshape(2, blk, N_META, -1).transpose(0, 3, 1, 2)
    bias_m = jnp.stack([bias_m[1], bias_m[0], bias_m[1], bias_m[0]])
    hq = bias_m.shape[1]
    sink_col = jnp.broadcast_to(sink.astype(F32)[None, :, None, None], (4, hq, blk, 1))
    pad = jnp.full((4, hq, blk, LANES - N_META - 1), NEG_INF, F32)
    return jnp.concatenate([bias_w, bias_m, sink_col, pad], axis=-1) * LOG2_E


def _trunk(x, meta_kv, tabs, params):
    batch, seq, d = x.shape
    (g_attn, w_in, head_gain, head_flag, w_out, g_mlp, w_up, w_down) = params
    km_na, vm_na, km_sw, vm_sw = meta_kv
    na_tiles, na_plans, swa_bias = tabs
    nh_na = km_na.shape[0]
    hq_sw = swa_bias.shape[1]
    hkv_sw = km_sw.shape[0]
    x2 = x.reshape(batch * seq, d)
    proj = _inproj(x2, g_attn, w_in, head_gain, head_flag, 512, 1024)
    rows = seq // GRID_W
    assert seq % (GRID_W * NA_QROWS) == 0 and rows >= NA_ROWS
    assert seq % SWA_BLOCK == 0
    na_o = _na_attention(proj, km_na, vm_na, na_tiles, na_plans, batch, rows)
    q_base = 3 * nh_na
    k_base = q_base + hq_sw
    v_base = k_base + hkv_sw
    sw_o = _swa_attention(proj, km_sw, vm_sw, swa_bias, batch, seq, q_base, k_base, v_base)
    x1, x1g, x1ss = _outproj(na_o, sw_o, w_out, x2, g_mlp, 512, 2048)
    u = _mlp_up(x1g, x1ss, w_up, 1024, 1024)
    y = _mlp_down(u, w_down, x1, 1024, 1024, 4096)
    return y.reshape(batch, seq, d)


def kernel(x_prompt, x_sample, meta_tokens, t5_bias, norm_attn, w_in, q_norm_na, k_norm_na, na_rpb,
           q_norm_swa, k_norm_swa, swa_sink, w_out, norm_mlp, w_up, w_down):
    depth = w_in.shape[0]
    assert depth == 1
    nh_na = na_rpb.shape[1]
    hq_sw = swa_sink.shape[1]
    in_width = w_in.shape[2]
    hkv_sw = (in_width // HEAD_DIM - 3 * nh_na - hq_sw) // 2

    ones = jnp.ones((HEAD_DIM,), F32)

    def rep(g, n):
        return jnp.broadcast_to(g.astype(F32)[None], (n, HEAD_DIM))

    head_gain = jnp.concatenate([
        rep(q_norm_na[0], nh_na), rep(k_norm_na[0], nh_na), rep(ones, nh_na),
        rep(q_norm_swa[0], hq_sw), rep(k_norm_swa[0], hkv_sw), rep(ones, hkv_sw)])[:, None, :]
    flag = np.concatenate([np.ones(2 * nh_na), np.zeros(nh_na), np.ones(hq_sw + hkv_sw),
                           np.zeros(hkv_sw)]).astype(np.float32)
    head_flag = jnp.asarray(np.broadcast_to(flag[:, None, None], (flag.shape[0], 1, HEAD_DIM)))

    w_in_b = w_in[0].astype(BF16)
    w_out_b = w_out[0].astype(BF16)
    w_up_f = w_up[0].astype(F32)
    w_down_b = w_down[0].astype(BF16)
    g_attn = norm_attn[0].astype(F32)[None]
    g_mlp = norm_mlp[0].astype(F32)[None]

    mproj = _inproj(meta_tokens.astype(F32), g_attn, w_in_b, head_gain, head_flag, N_META, 512)
    k0 = 3 * nh_na + hq_sw
    mpad = jnp.pad(mproj[k0:k0 + 2 * hkv_sw], ((0, 0), (0, LANES - N_META), (0, 0)))
    meta_kv = (mproj[nh_na:2 * nh_na], mproj[2 * nh_na:3 * nh_na], mpad[:hkv_sw], mpad[hkv_sw:])

    specs, plans = _na_plan()
    tabs = (_na_tiles(na_rpb[0], specs), plans, _swa_tables(t5_bias, swa_sink[0]))
    params = (g_attn, w_in_b, head_gain, head_flag, w_out_b, g_mlp, w_up_f, w_down_b)
    y_prompt = _trunk(x_prompt, meta_kv, tabs, params)
    y_sample = _trunk(x_sample, meta_kv, tabs, params)
    return (y_prompt, y_sample)
```

```python
import functools
import math

import jax
import jax.numpy as jnp
import numpy as np
from jax import lax
from jax.experimental import pallas as pl
from jax.experimental.pallas import tpu as pltpu

F32 = jnp.float32
BF16 = jnp.bfloat16

HEAD_DIM = 128
N_META = 16
GRID_W = 64
NA_ROWS = 8
NA_COLS = 16
SWA_WINDOW = 128
SWA_BLOCK = 128
T5_BUCKETS = 32
T5_MAX_DIST = 128
NORM_EPS = 1e-6
NEG_INF = -1e30
LOG2_E = math.log2(math.e)

LANES = 128
NA_QROWS = 4
NA_QBLK = NA_QROWS * GRID_W
NA_KBLKS = 3
NA_HEADS_PER_STEP = 8
SWA_KV_HEADS_PER_STEP = 4
NA_TILES_PER_ROW = NA_KBLKS * NA_QBLK // LANES
MIB = 1024 * 1024
V7X_VMEM_MIB = 64
BF16_SUBLANES = 16

TILES = {
    "inproj": dict(bm=512, bn=1024, vmem=60),
    "outproj": dict(bm=512, bn=2048, vmem=56),
    "mlp_up": dict(bm=1024, bn=1024, vmem=56),
    "mlp_down": dict(bm=1024, bn=1024, bk=4096, vmem=60),
}
ATTN_VMEM_MIB = dict(na=48, swa=32)

assert 2 * GRID_W == LANES
assert all(t["vmem"] < V7X_VMEM_MIB for t in TILES.values())


def _cparams(sem, vmem_mib):
    return pltpu.CompilerParams(dimension_semantics=sem, vmem_limit_bytes=vmem_mib * MIB)


def _rms_rows(x, gain):
    ms = jnp.mean(x * x, axis=-1, keepdims=True)
    return x * lax.rsqrt(ms + NORM_EPS) * gain


def _inproj_kernel(x_ref, g_ref, w_ref, hg_ref, hf_ref, *refs, n_cast):
    cast_in, o_ref, cast_out, xn_ref = (refs[:n_cast], refs[n_cast],
                                        refs[n_cast + 1:2 * n_cast + 1], refs[-1])
    for src_ref, dst_ref in zip(cast_in, cast_out):
        dst_ref[...] = src_ref[...].astype(BF16)

    @pl.when(pl.program_id(1) == 0)
    def _():
        xn_ref[...] = _rms_rows(x_ref[...], g_ref[...]).astype(BF16)

    res = jnp.dot(xn_ref[...], w_ref[...], preferred_element_type=F32)
    for c in range(o_ref.shape[0]):
        blk = res[:, c * LANES:(c + 1) * LANES]
        r = lax.rsqrt(jnp.mean(blk * blk, axis=-1, keepdims=True) + NORM_EPS)
        f = hf_ref[c]
        o_ref[c] = (blk * (f * r + (1.0 - f)) * hg_ref[c]).astype(BF16)


def _inproj(x2, gain, w, head_gain, head_flag, bm, casts=()):
    m, d = x2.shape
    n = w.shape[1]
    bn, vmem = TILES["inproj"]["bn"], TILES["inproj"]["vmem"]
    cpb = bn // LANES
    ni, nj = m // bm, n // bn
    cast_specs, cast_shapes = [], []
    for wc in casts:
        rows, cols = wc.shape
        nchunks = 1 << ((ni * nj).bit_length() - 1)
        rpc = rows // nchunks
        assert rows % nchunks == 0 and rpc % BF16_SUBLANES == 0
        cast_specs.append(pl.BlockSpec(
            (rpc, cols), lambda i, j, nchunks=nchunks: (jnp.minimum(i * nj + j, nchunks - 1), 0)))
        cast_shapes.append(jax.ShapeDtypeStruct((rows, cols), BF16))
    outs = pl.pallas_call(
        functools.partial(_inproj_kernel, n_cast=len(casts)),
        grid=(ni, nj),
        in_specs=[
            pl.BlockSpec((bm, d), lambda i, j: (i, 0)),
            pl.BlockSpec((1, d), lambda i, j: (0, 0)),
            pl.BlockSpec((d, bn), lambda i, j: (0, j)),
            pl.BlockSpec((cpb, 1, LANES), lambda i, j: (j, 0, 0)),
            pl.BlockSpec((cpb, 1, LANES), lambda i, j: (j, 0, 0)),
        ] + cast_specs,
        out_specs=[pl.BlockSpec((cpb, bm, LANES), lambda i, j: (j, i, 0))] + cast_specs,
        out_shape=[jax.ShapeDtypeStruct((n // LANES, m, LANES), BF16)] + cast_shapes,
        scratch_shapes=[pltpu.VMEM((bm, d), BF16)],
        compiler_params=_cparams(("arbitrary", "arbitrary"), vmem),
        name="inproj",
    )(x2, gain, w, head_gain, head_flag, *casts)
    return outs


def _mlp_up_kernel(a_ref, ss_ref, w_ref, o_ref, *, d):
    ss = ss_ref[0][:, :1]
    for c in range(1, ss_ref.shape[0]):
        ss = ss + ss_ref[c][:, :1]
    r = lax.rsqrt(ss * (1.0 / d) + NORM_EPS)
    u = jnp.dot(a_ref[...], w_ref[...], preferred_element_type=F32) * r
    u = jnp.maximum(u, 0.0)
    o_ref[...] = (u * u).astype(BF16)


def _mlp_up(a, ss, w):
    m, d = a.shape
    n = w.shape[1]
    ncol = ss.shape[0]
    bm, bn, vmem = (TILES["mlp_up"][k] for k in ("bm", "bn", "vmem"))
    return pl.pallas_call(
        functools.partial(_mlp_up_kernel, d=d),
        grid=(m // bm, n // bn),
        in_specs=[
            pl.BlockSpec((bm, d), lambda i, j: (i, 0), pipeline_mode=pl.Buffered(1)),
            pl.BlockSpec((ncol, bm, LANES), lambda i, j: (0, i, 0)),
            pl.BlockSpec((d, bn), lambda i, j: (0, j)),
        ],
        out_specs=pl.BlockSpec((bm, bn), lambda i, j: (i, j)),
        out_shape=jax.ShapeDtypeStruct((m, n), BF16),
        compiler_params=_cparams(("parallel", "parallel"), vmem),
        name="mlp_up",
    )(a, ss, w)


def _mlp_down_kernel(a_ref, w_ref, r_ref, o_ref):
    @pl.when(pl.program_id(2) == 0)
    def _():
        o_ref[...] = r_ref[...]

    o_ref[...] += jnp.dot(a_ref[...], w_ref[...], preferred_element_type=F32)


def _mlp_down(a, w, resid):
    m, kd = a.shape
    n = w.shape[1]
    bm, bn, bk, vmem = (TILES["mlp_down"][k] for k in ("bm", "bn", "bk", "vmem"))
    return pl.pallas_call(
        _mlp_down_kernel,
        grid=(m // bm, n // bn, kd // bk),
        in_specs=[
            pl.BlockSpec((bm, bk), lambda i, j, k: (i, k)),
            pl.BlockSpec((bk, bn), lambda i, j, k: (k, j)),
            pl.BlockSpec((bm, bn), lambda i, j, k: (i, j)),
        ],
        out_specs=pl.BlockSpec((bm, bn), lambda i, j, k: (i, j)),
        out_shape=jax.ShapeDtypeStruct((m, n), F32),
        compiler_params=_cparams(("parallel", "parallel", "arbitrary"), vmem),
        name="mlp_down",
    )(a, w, resid)


def _outproj_kernel(na_ref, sw_ref, wt_ref, wb_ref, r_ref, g_ref, o_ref, ob_ref, ss_ref):
    acc = jnp.dot(na_ref[...], wt_ref[...], preferred_element_type=F32)
    acc = acc + jnp.dot(sw_ref[...], wb_ref[...], preferred_element_type=F32)
    x1 = r_ref[...] + acc
    o_ref[...] = x1
    ob_ref[...] = (x1 * g_ref[...]).astype(BF16)
    ss_ref[...] = jnp.broadcast_to(jnp.sum(x1 * x1, axis=-1, keepdims=True), ss_ref.shape)


def _outproj(na_o, sw_o, w, resid, gain):
    m, kh = na_o.shape
    n = w.shape[1]
    bm, bn, vmem = (TILES["outproj"][k] for k in ("bm", "bn", "vmem"))
    return pl.pallas_call(
        _outproj_kernel,
        grid=(n // bn, m // bm),
        in_specs=[
            pl.BlockSpec((bm, kh), lambda j, i: (i, 0)),
            pl.BlockSpec((bm, kh), lambda j, i: (i, 0)),
            pl.BlockSpec((kh, bn), lambda j, i: (0, j), pipeline_mode=pl.Buffered(1)),
            pl.BlockSpec((kh, bn), lambda j, i: (1, j), pipeline_mode=pl.Buffered(1)),
            pl.BlockSpec((bm, bn), lambda j, i: (i, j)),
            pl.BlockSpec((1, bn), lambda j, i: (0, j)),
        ],
        out_specs=[pl.BlockSpec((bm, bn), lambda j, i: (i, j)),
                   pl.BlockSpec((bm, bn), lambda j, i: (i, j)),
                   pl.BlockSpec((None, bm, LANES), lambda j, i: (j, i, 0))],
        out_shape=[jax.ShapeDtypeStruct((m, n), F32),
                   jax.ShapeDtypeStruct((m, n), BF16),
                   jax.ShapeDtypeStruct((n // bn, m, LANES), F32)],
        compiler_params=_cparams(("parallel", "parallel"), vmem),
        name="outproj",
    )(na_o, sw_o, w, w, resid, gain)


def _qkt(q, k):
    return lax.dot_general(q, k, (((1,), (1,)), ((), ())), preferred_element_type=F32)


def _softmax_pv(s, v):
    p = jnp.exp2(s - jnp.max(s, axis=-1, keepdims=True)).astype(BF16)
    o = jnp.dot(p, jnp.concatenate([v, jnp.ones_like(v)], axis=1), preferred_element_type=F32)
    return (o[:, :LANES] / o[:, LANES:]).astype(BF16)


def _na_plan():
    vblk = 6
    vrows = vblk * NA_QROWS
    wrows = NA_KBLKS * NA_QROWS
    specs, plans = [], []
    for j in (0, 2, vblk - 1):
        ws = int(np.clip(j - 1, 0, vblk - NA_KBLKS))
        meta_row = 0 if j == vblk - 1 else wrows - 1
        plan = []
        for qrl in range(NA_QROWS):
            qr = NA_QROWS * j + qrl
            rs = int(np.clip(qr - NA_ROWS // 2, 0, vrows - NA_ROWS))
            row = []
            for t in range(NA_TILES_PER_ROW):
                halves = []
                for krl in (2 * t, 2 * t + 1):
                    kr = NA_QROWS * ws + krl
                    half = kr - qr if rs <= kr < rs + NA_ROWS else None
                    if krl == meta_row:
                        assert half is None
                        half = "meta"
                    halves.append(half)
                halves = tuple(halves)
                if halves == (None, None):
                    row.append(-1)
                else:
                    if halves not in specs:
                        specs.append(halves)
                    row.append(specs.index(halves))
            plan.append(row)
        plans.append(plan)
    return specs, plans


def _na_tiles(rpb, specs):
    nh = rpb.shape[0]
    c = np.arange(GRID_W)
    cs = np.clip(c - NA_COLS // 2, 0, GRID_W - NA_COLS)
    col_in = (c[None, :] >= cs[:, None]) & (c[None, :] < cs[:, None] + NA_COLS)
    dc = np.clip(c[None, :] - c[:, None], -(NA_COLS - 1), NA_COLS - 1) + (NA_COLS - 1)
    onehot = np.zeros((2 * NA_COLS - 1, GRID_W * GRID_W), np.float32)
    onehot[dc.reshape(-1), np.arange(GRID_W * GRID_W)] = 1.0
    tcol = jnp.einsum("hab,bx->hax", rpb.astype(F32), onehot, precision=lax.Precision.HIGHEST)
    tcol = jnp.where(col_in[None, None], tcol.reshape(nh, 2 * NA_ROWS - 1, GRID_W, GRID_W), NEG_INF)
    masked = jnp.full((nh, GRID_W, GRID_W), NEG_INF, F32)
    meta = jnp.where(np.arange(GRID_W) < N_META, 0.0, masked)

    def half(dr):
        if dr is None:
            return masked
        return meta if dr == "meta" else tcol[:, dr + NA_ROWS - 1]

    tiles = jnp.stack([jnp.concatenate([half(l), half(r)], axis=-1) for l, r in specs], axis=1)
    return tiles * LOG2_E


def _na_kernel(q_ref, k0_ref, k1_ref, k2_ref, v0_ref, v1_ref, v2_ref, km_ref, vm_ref, tiles_ref,
               o_ref, bias_ref, k_scr, v_scr, *, scale, plans):
    hb = q_ref.shape[0]
    j = pl.program_id(2)
    last = pl.num_programs(2) - 1
    masked_tile = jnp.full((GRID_W, LANES), NEG_INF, F32)

    def build(plan):
        def body():
            for hh in range(hb):
                for qrl in range(NA_QROWS):
                    for t in range(NA_TILES_PER_ROW):
                        idx = plan[qrl][t]
                        tile = masked_tile if idx < 0 else tiles_ref[hh, idx]
                        bias_ref[hh, qrl * GRID_W:(qrl + 1) * GRID_W, t * LANES:(t + 1) * LANES] = tile
        return body

    pl.when(j == 0)(build(plans[0]))
    pl.when(j == 1)(build(plans[1]))
    pl.when(j == last)(build(plans[2]))

    meta_pos = pl.multiple_of(
        jnp.where(j == last, 0, (NA_KBLKS * NA_QROWS - 1) * GRID_W), N_META)
    for hh in range(hb):
        for d, (kb_ref, vb_ref) in enumerate(((k0_ref, v0_ref), (k1_ref, v1_ref), (k2_ref, v2_ref))):
            k_scr[hh, d * NA_QBLK:(d + 1) * NA_QBLK, :] = kb_ref[hh]
            v_scr[hh, d * NA_QBLK:(d + 1) * NA_QBLK, :] = vb_ref[hh]
        k_scr[hh, pl.ds(meta_pos, N_META), :] = km_ref[hh]
        v_scr[hh, pl.ds(meta_pos, N_META), :] = vm_ref[hh]
        q = q_ref[hh]
        k = k_scr[hh]
        v = v_scr[hh]
        s = _qkt(q, k) * (scale * LOG2_E) + bias_ref[hh]
        o_ref[:, hh * LANES:(hh + 1) * LANES] = _softmax_pv(s, v)


def _na_attention(proj, km, vm, tiles, plans, batch, rows):
    nh = km.shape[0]
    hb = NA_HEADS_PER_STEP
    nblk = rows // NA_QROWS
    tokens = proj.shape[1]
    assert nblk >= NA_KBLKS and nh % hb == 0
    hblks = nh // hb

    def kv_map(base, d):
        def f(h, b, j):
            ws = jnp.clip(j - 1, 0, nblk - NA_KBLKS)
            return (base + h, b * nblk + ws + d, 0)
        return f

    qspec = pl.BlockSpec((hb, NA_QBLK, LANES), lambda h, b, j: (h, b * nblk + j, 0))
    kspecs = [pl.BlockSpec((hb, NA_QBLK, LANES), kv_map(hblks, d)) for d in range(NA_KBLKS)]
    vspecs = [pl.BlockSpec((hb, NA_QBLK, LANES), kv_map(2 * hblks, d)) for d in range(NA_KBLKS)]
    mspec = pl.BlockSpec((hb, N_META, LANES), lambda h, b, j: (h, 0, 0))
    tspec = pl.BlockSpec((hb,) + tiles.shape[1:], lambda h, b, j: (h, 0, 0, 0))
    return pl.pallas_call(
        functools.partial(_na_kernel, scale=HEAD_DIM ** -0.5, plans=plans),
        grid=(hblks, batch, nblk),
        in_specs=[qspec] + kspecs + vspecs + [mspec, mspec, tspec],
        out_specs=pl.BlockSpec((NA_QBLK, hb * LANES), lambda h, b, j: (b * nblk + j, h)),
        out_shape=jax.ShapeDtypeStruct((tokens, nh * LANES), BF16),
        scratch_shapes=[pltpu.VMEM((hb, NA_QBLK, NA_KBLKS * NA_QBLK), F32),
                        pltpu.VMEM((hb, NA_KBLKS * NA_QBLK, LANES), BF16),
                        pltpu.VMEM((hb, NA_KBLKS * NA_QBLK, LANES), BF16)],
        compiler_params=_cparams(("arbitrary", "arbitrary", "arbitrary"), ATTN_VMEM_MIB["na"]),
        name="na_attn",
    )(proj, proj, proj, proj, proj, proj, proj, km, vm, tiles)


def _swa_kernel(q_ref, k0_ref, k1_ref, k2_ref, v0_ref, v1_ref, v2_ref, km_ref, vm_ref, bias_ref,
                o_ref, *, scale, group):
    hkb = k0_ref.shape[0]
    blk = q_ref.shape[1]
    for kh in range(hkb):
        heads = slice(kh * group, (kh + 1) * group)
        q = q_ref[heads].reshape(group * blk, LANES)
        k = jnp.concatenate([k0_ref[kh], k1_ref[kh], k2_ref[kh], km_ref[kh]], axis=0)
        v = jnp.concatenate([v0_ref[kh], v1_ref[kh], v2_ref[kh], vm_ref[kh]], axis=0)
        s = _qkt(q, k) * (scale * LOG2_E) + bias_ref[heads].reshape(group * blk, 3 * blk + LANES)
        o = _softmax_pv(s, v)
        for g in range(group):
            col = (kh * group + g) * LANES
            o_ref[:, col:col + LANES] = o[g * blk:(g + 1) * blk]


def _swa_attention(proj, km, vm, bias, batch, seq, q_base, k_base, v_base):
    hkv = km.shape[0]
    hq = bias.shape[1]
    group = hq // hkv
    hkb = SWA_KV_HEADS_PER_STEP
    blk = SWA_BLOCK
    nb = seq // blk
    tokens = proj.shape[1]
    assert hkv % hkb == 0 and k_base % hkb == 0 and v_base % hkb == 0 and q_base % (hkb * group) == 0

    def bias_map(hk, b, i):
        variant = jnp.where(i == 0, 1, 0) + jnp.where(i == nb - 1, 2, 0)
        return (variant, hk, 0, 0)

    def kv_map(base, d):
        def f(hk, b, i):
            return (base // hkb + hk, b * nb + jnp.clip(i - 1 + d, 0, nb - 1), 0)
        return f

    kspecs = [pl.BlockSpec((hkb, blk, LANES), kv_map(k_base, d)) for d in range(3)]
    vspecs = [pl.BlockSpec((hkb, blk, LANES), kv_map(v_base, d)) for d in range(3)]
    mspec = pl.BlockSpec((hkb, LANES, LANES), lambda hk, b, i: (hk, 0, 0))
    return pl.pallas_call(
        functools.partial(_swa_kernel, scale=HEAD_DIM ** -0.5, group=group),
        grid=(hkv // hkb, batch, nb),
        in_specs=[pl.BlockSpec((hkb * group, blk, LANES),
                               lambda hk, b, i: (q_base // (hkb * group) + hk, b * nb + i, 0))]
                 + kspecs + vspecs + [mspec, mspec,
                  pl.BlockSpec((None, hkb * group, blk, 3 * blk + LANES), bias_map)],
        out_specs=pl.BlockSpec((blk, hkb * group * LANES), lambda hk, b, i: (b * nb + i, hk)),
        out_shape=jax.ShapeDtypeStruct((tokens, hq * LANES), BF16),
        compiler_params=_cparams(("parallel", "parallel", "parallel"), ATTN_VMEM_MIB["swa"]),
        name="swa_attn",
    )(proj, proj, proj, proj, proj, proj, proj, km, vm, bias)


def _t5_bucket(rel):
    nb = T5_BUCKETS // 2
    max_exact = nb // 2
    ret = np.where(rel > 0, nb, 0)
    n = np.abs(rel)
    large = max_exact + (np.log(np.maximum(n, 1) / max_exact)
                         / math.log(T5_MAX_DIST / max_exact) * (nb - max_exact)).astype(np.int64)
    large = np.minimum(large, nb - 1)
    return ret + np.where(n < max_exact, n, large)


def _select_rows(table, idx):
    flat = idx.reshape(-1)
    onehot = np.zeros((table.shape[0], flat.size), np.float32)
    onehot[flat, np.arange(flat.size)] = 1.0
    out = jnp.einsum("bh,bx->xh", table.astype(F32), onehot, precision=lax.Precision.HIGHEST)
    return out.reshape(idx.shape + (table.shape[1],))


def _swa_tables(t5_bias, sink):
    blk = SWA_BLOCK
    rel = np.arange(-(2 * blk - 1), 2 * blk)
    by_rel = _select_rows(t5_bias, _t5_bucket(rel)).T
    by_rel = jnp.where((np.abs(rel) <= SWA_WINDOW)[None], by_rel, NEG_INF)
    bias_w = jnp.stack([by_rel[:, blk - 1 - q:blk - 1 - q + 3 * blk] for q in range(blk)], axis=1)
    jj = np.arange(3 * blk)
    edge = np.stack([np.zeros_like(jj, bool), jj < blk, jj >= 2 * blk, (jj < blk) | (jj >= 2 * blk)])
    bias_w = jnp.where(edge[:, None, None, :], NEG_INF, bias_w[None])
    assert N_META + blk - (N_META - 1) > T5_MAX_DIST
    rel_m = np.arange(N_META)[None, :] - (N_META + np.arange(2 * blk))[:, None]
    bias_m = _select_rows(t5_bias, _t5_bucket(rel_m))
    bias_m = bias_m.reshape(2, blk, N_META, -1).transpose(0, 3, 1, 2)
    bias_m = jnp.stack([bias_m[1], bias_m[0], bias_m[1], bias_m[0]])
    hq = bias_m.shape[1]
    sink_col = jnp.broadcast_to(sink.astype(F32)[None, :, None, None], (4, hq, blk, 1))
    pad = jnp.full((4, hq, blk, LANES - N_META - 1), NEG_INF, F32)
    return jnp.concatenate([bias_w, bias_m, sink_col, pad], axis=-1) * LOG2_E


def _trunk(x, meta_kv, tabs, params, w_out, w_down):
    batch, seq, d = x.shape
    (g_attn, w_in, head_gain, head_flag, g_mlp, w_up) = params
    km_na, vm_na, km_sw, vm_sw = meta_kv
    na_tiles, na_plans, swa_bias = tabs
    nh_na = km_na.shape[0]
    hq_sw = swa_bias.shape[1]
    hkv_sw = km_sw.shape[0]
    x2 = x.reshape(batch * seq, d)
    casts = () if w_out.dtype == BF16 else (w_out, w_down)
    proj, *cast_out = _inproj(x2, g_attn, w_in, head_gain, head_flag, TILES["inproj"]["bm"], casts)
    if cast_out:
        w_out, w_down = cast_out
    rows = seq // GRID_W
    assert seq % (GRID_W * NA_QROWS) == 0 and rows >= NA_ROWS
    assert seq % SWA_BLOCK == 0
    na_o = _na_attention(proj, km_na, vm_na, na_tiles, na_plans, batch, rows)
    q_base = 3 * nh_na
    k_base = q_base + hq_sw
    v_base = k_base + hkv_sw
    sw_o = _swa_attention(proj, km_sw, vm_sw, swa_bias, batch, seq, q_base, k_base, v_base)
    x1, x1g, x1ss = _outproj(na_o, sw_o, w_out, x2, g_mlp)
    u = _mlp_up(x1g, x1ss, w_up)
    y = _mlp_down(u, w_down, x1)
    return y.reshape(batch, seq, d), w_out, w_down


def kernel(x_prompt, x_sample, meta_tokens, t5_bias, norm_attn, w_in, q_norm_na, k_norm_na, na_rpb,
           q_norm_swa, k_norm_swa, swa_sink, w_out, norm_mlp, w_up, w_down):
    depth = w_in.shape[0]
    assert depth == 1
    nh_na = na_rpb.shape[1]
    hq_sw = swa_sink.shape[1]
    in_width = w_in.shape[2]
    hkv_sw = (in_width // HEAD_DIM - 3 * nh_na - hq_sw) // 2

    ones = jnp.ones((HEAD_DIM,), F32)

    def rep(g, n):
        return jnp.broadcast_to(g.astype(F32)[None], (n, HEAD_DIM))

    head_gain = jnp.concatenate([
        rep(q_norm_na[0], nh_na), rep(k_norm_na[0], nh_na), rep(ones, nh_na),
        rep(q_norm_swa[0], hq_sw), rep(k_norm_swa[0], hkv_sw), rep(ones, hkv_sw)])[:, None, :]
    flag = np.concatenate([np.ones(2 * nh_na), np.zeros(nh_na), np.ones(hq_sw + hkv_sw),
                           np.zeros(hkv_sw)]).astype(np.float32)
    head_flag = jnp.asarray(np.broadcast_to(flag[:, None, None], (flag.shape[0], 1, HEAD_DIM)))

    w_in_b = w_in[0].astype(BF16)
    g_attn = norm_attn[0].astype(F32)[None]
    g_mlp = norm_mlp[0].astype(F32)[None]

    mproj, = _inproj(meta_tokens.astype(F32), g_attn, w_in_b, head_gain, head_flag, N_META)
    k0 = 3 * nh_na + hq_sw
    mpad = jnp.pad(mproj[k0:k0 + 2 * hkv_sw], ((0, 0), (0, LANES - N_META), (0, 0)))
    meta_kv = (mproj[nh_na:2 * nh_na], mproj[2 * nh_na:3 * nh_na], mpad[:hkv_sw], mpad[hkv_sw:])

    specs, plans = _na_plan()
    tabs = (_na_tiles(na_rpb[0], specs), plans, _swa_tables(t5_bias, swa_sink[0]))
    params = (g_attn, w_in_b, head_gain, head_flag, g_mlp, w_up[0].astype(F32))
    y_prompt, w_out_b, w_down_b = _trunk(x_prompt, meta_kv, tabs, params,
                                         w_out[0].astype(F32), w_down[0].astype(F32))
    y_sample, _, _ = _trunk(x_sample, meta_kv, tabs, params, w_out_b, w_down_b)
    return (y_prompt, y_sample)
```

```python
import functools
import math

import jax
import jax.numpy as jnp
import numpy as np
from jax import lax
from jax.experimental import pallas as pl
from jax.experimental.pallas import tpu as pltpu

F32 = jnp.float32
BF16 = jnp.bfloat16

HEAD_DIM = 128
N_META = 16
GRID_W = 64
NA_ROWS = 8
NA_COLS = 16
SWA_WINDOW = 128
SWA_BLOCK = 128
T5_BUCKETS = 32
T5_MAX_DIST = 128
NORM_EPS = 1e-6
NEG_INF = -1e30
LOG2_E = math.log2(math.e)

LANES = 128
NA_QROWS = 4
NA_QBLK = NA_QROWS * GRID_W
NA_KBLKS = 3
NA_HEADS_PER_STEP = 8
SWA_KV_HEADS_PER_STEP = 4
NA_TILES_PER_ROW = NA_KBLKS * NA_QBLK // LANES
MIB = 1024 * 1024
V7X_VMEM_MIB = 64
BF16_SUBLANES = 16

TILES = {
    "inproj": dict(bm=512, bn=1024, vmem=60),
    "outproj": dict(bm=512, bn=2048, vmem=56),
    "mlp_up": dict(bm=1024, bn=1024, vmem=52),
    "mlp_down": dict(bm=1024, bn=1024, bk=4096, vmem=60),
}
ATTN_VMEM_MIB = dict(na=48, swa=32)

assert 2 * GRID_W == LANES
assert all(t["vmem"] < V7X_VMEM_MIB for t in TILES.values())


def _cparams(sem, vmem_mib):
    return pltpu.CompilerParams(dimension_semantics=sem, vmem_limit_bytes=vmem_mib * MIB)


def _rms_rows(x, gain):
    ms = jnp.mean(x * x, axis=-1, keepdims=True)
    return x * lax.rsqrt(ms + NORM_EPS) * gain


def _inproj_kernel(x_ref, g_ref, w_ref, hg_ref, hf_ref, *refs, n_cast):
    cast_in, o_ref, cast_out, xn_ref = (refs[:n_cast], refs[n_cast],
                                        refs[n_cast + 1:2 * n_cast + 1], refs[-1])
    for src_ref, dst_ref in zip(cast_in, cast_out):
        dst_ref[...] = src_ref[...].astype(BF16)

    @pl.when(pl.program_id(1) == 0)
    def _():
        xn_ref[...] = _rms_rows(x_ref[...], g_ref[...]).astype(BF16)

    res = jnp.dot(xn_ref[...], w_ref[...], preferred_element_type=F32)
    for c in range(o_ref.shape[0]):
        blk = res[:, c * LANES:(c + 1) * LANES]
        r = lax.rsqrt(jnp.mean(blk * blk, axis=-1, keepdims=True) + NORM_EPS)
        f = hf_ref[c]
        o_ref[c] = (blk * (f * r + (1.0 - f)) * hg_ref[c]).astype(BF16)


def _inproj(x2, gain, w, head_gain, head_flag, bm, casts=()):
    m, d = x2.shape
    n = w.shape[1]
    bn, vmem = TILES["inproj"]["bn"], TILES["inproj"]["vmem"]
    cpb = bn // LANES
    ni, nj = m // bm, n // bn
    cast_specs, cast_shapes = [], []
    for wc in casts:
        rows, cols = wc.shape
        nchunks = 1 << ((ni * nj).bit_length() - 1)
        rpc = rows // nchunks
        assert rows % nchunks == 0 and rpc % BF16_SUBLANES == 0
        cast_specs.append(pl.BlockSpec(
            (rpc, cols), lambda i, j, nchunks=nchunks: (jnp.minimum(i * nj + j, nchunks - 1), 0)))
        cast_shapes.append(jax.ShapeDtypeStruct((rows, cols), BF16))
    outs = pl.pallas_call(
        functools.partial(_inproj_kernel, n_cast=len(casts)),
        grid=(ni, nj),
        in_specs=[
            pl.BlockSpec((bm, d), lambda i, j: (i, 0)),
            pl.BlockSpec((1, d), lambda i, j: (0, 0)),
            pl.BlockSpec((d, bn), lambda i, j: (0, j)),
            pl.BlockSpec((cpb, 1, LANES), lambda i, j: (j, 0, 0)),
            pl.BlockSpec((cpb, 1, LANES), lambda i, j: (j, 0, 0)),
        ] + cast_specs,
        out_specs=[pl.BlockSpec((cpb, bm, LANES), lambda i, j: (j, i, 0))] + cast_specs,
        out_shape=[jax.ShapeDtypeStruct((n // LANES, m, LANES), BF16)] + cast_shapes,
        scratch_shapes=[pltpu.VMEM((bm, d), BF16)],
        compiler_params=_cparams(("arbitrary", "arbitrary"), vmem),
        name="inproj",
    )(x2, gain, w, head_gain, head_flag, *casts)
    return outs


def _mlp_up_kernel(a_ref, ss_ref, w_ref, o_ref, *, d):
    ss = ss_ref[0][:, :1]
    for c in range(1, ss_ref.shape[0]):
        ss = ss + ss_ref[c][:, :1]
    r = lax.rsqrt(ss * (1.0 / d) + NORM_EPS)
    u = jnp.dot(a_ref[...], w_ref[...], preferred_element_type=F32) * r
    u = jnp.maximum(u, 0.0)
    o_ref[...] = (u * u).astype(BF16)


def _mlp_up(a, ss, w):
    m, d = a.shape
    n = w.shape[1]
    ncol = ss.shape[0]
    bm, bn, vmem = (TILES["mlp_up"][k] for k in ("bm", "bn", "vmem"))
    return pl.pallas_call(
        functools.partial(_mlp_up_kernel, d=d),
        grid=(m // bm, n // bn),
        in_specs=[
            pl.BlockSpec((bm, d), lambda i, j: (i, 0)),
            pl.BlockSpec((ncol, bm, LANES), lambda i, j: (0, i, 0)),
            pl.BlockSpec((d, bn), lambda i, j: (0, j)),
        ],
        out_specs=pl.BlockSpec((bm, bn), lambda i, j: (i, j)),
        out_shape=jax.ShapeDtypeStruct((m, n), BF16),
        compiler_params=_cparams(("parallel", "parallel"), vmem),
        name="mlp_up",
    )(a, ss, w)


def _mlp_down_kernel(a_ref, w_ref, r_ref, o_ref):
    @pl.when(pl.program_id(2) == 0)
    def _():
        o_ref[...] = r_ref[...]

    o_ref[...] += jnp.dot(a_ref[...], w_ref[...], preferred_element_type=F32)


def _mlp_down(a, w, resid):
    m, kd = a.shape
    n = w.shape[1]
    bm, bn, bk, vmem = (TILES["mlp_down"][k] for k in ("bm", "bn", "bk", "vmem"))
    return pl.pallas_call(
        _mlp_down_kernel,
        grid=(m // bm, n // bn, kd // bk),
        in_specs=[
            pl.BlockSpec((bm, bk), lambda i, j, k: (i, k)),
            pl.BlockSpec((bk, bn), lambda i, j, k: (k, j)),
            pl.BlockSpec((bm, bn), lambda i, j, k: (i, j)),
        ],
        out_specs=pl.BlockSpec((bm, bn), lambda i, j, k: (i, j)),
        out_shape=jax.ShapeDtypeStruct((m, n), F32),
        compiler_params=_cparams(("parallel", "parallel", "arbitrary"), vmem),
        name="mlp_down",
    )(a, w, resid)


def _outproj_kernel(na_ref, sw_ref, wt_ref, wb_ref, r_ref, g_ref, o_ref, ob_ref, ss_ref):
    acc = jnp.dot(na_ref[...], wt_ref[...], preferred_element_type=F32)
    acc = acc + jnp.dot(sw_ref[...], wb_ref[...], preferred_element_type=F32)
    x1 = r_ref[...] + acc
    o_ref[...] = x1
    ob_ref[...] = (x1 * g_ref[...]).astype(BF16)
    ss_ref[...] = jnp.broadcast_to(jnp.sum(x1 * x1, axis=-1, keepdims=True), ss_ref.shape)


def _outproj(na_o, sw_o, w, resid, gain):
    m, kh = na_o.shape
    n = w.shape[1]
    bm, bn, vmem = (TILES["outproj"][k] for k in ("bm", "bn", "vmem"))
    return pl.pallas_call(
        _outproj_kernel,
        grid=(n // bn, m // bm),
        in_specs=[
            pl.BlockSpec((bm, kh), lambda j, i: (i, 0)),
            pl.BlockSpec((bm, kh), lambda j, i: (i, 0)),
            pl.BlockSpec((kh, bn), lambda j, i: (0, j), pipeline_mode=pl.Buffered(1)),
            pl.BlockSpec((kh, bn), lambda j, i: (1, j), pipeline_mode=pl.Buffered(1)),
            pl.BlockSpec((bm, bn), lambda j, i: (i, j)),
            pl.BlockSpec((1, bn), lambda j, i: (0, j)),
        ],
        out_specs=[pl.BlockSpec((bm, bn), lambda j, i: (i, j)),
                   pl.BlockSpec((bm, bn), lambda j, i: (i, j)),
                   pl.BlockSpec((None, bm, LANES), lambda j, i: (j, i, 0))],
        out_shape=[jax.ShapeDtypeStruct((m, n), F32),
                   jax.ShapeDtypeStruct((m, n), BF16),
                   jax.ShapeDtypeStruct((n // bn, m, LANES), F32)],
        compiler_params=_cparams(("parallel", "parallel"), vmem),
        name="outproj",
    )(na_o, sw_o, w, w, resid, gain)


def _qkt(q, k):
    return lax.dot_general(q, k, (((1,), (1,)), ((), ())), preferred_element_type=F32)


def _softmax_pv(s, v):
    p = jnp.exp2(s - jnp.max(s, axis=-1, keepdims=True)).astype(BF16)
    o = jnp.dot(p, jnp.concatenate([v, jnp.ones_like(v)], axis=1), preferred_element_type=F32)
    return (o[:, :LANES] / o[:, LANES:]).astype(BF16)


def _na_plan():
    vblk = 6
    vrows = vblk * NA_QROWS
    wrows = NA_KBLKS * NA_QROWS
    specs, plans = [], []
    for j in (0, 2, vblk - 1):
        ws = int(np.clip(j - 1, 0, vblk - NA_KBLKS))
        meta_row = 0 if j == vblk - 1 else wrows - 1
        plan = []
        for qrl in range(NA_QROWS):
            qr = NA_QROWS * j + qrl
            rs = int(np.clip(qr - NA_ROWS // 2, 0, vrows - NA_ROWS))
            row = []
            for t in range(NA_TILES_PER_ROW):
                halves = []
                for krl in (2 * t, 2 * t + 1):
                    kr = NA_QROWS * ws + krl
                    half = kr - qr if rs <= kr < rs + NA_ROWS else None
                    if krl == meta_row:
                        assert half is None
                        half = "meta"
                    halves.append(half)
                halves = tuple(halves)
                if halves == (None, None):
                    row.append(-1)
                else:
                    if halves not in specs:
                        specs.append(halves)
                    row.append(specs.index(halves))
            plan.append(row)
        plans.append(plan)
    return specs, plans


def _na_tiles(rpb, specs):
    nh = rpb.shape[0]
    c = np.arange(GRID_W)
    cs = np.clip(c - NA_COLS // 2, 0, GRID_W - NA_COLS)
    col_in = (c[None, :] >= cs[:, None]) & (c[None, :] < cs[:, None] + NA_COLS)
    dc = np.clip(c[None, :] - c[:, None], -(NA_COLS - 1), NA_COLS - 1) + (NA_COLS - 1)
    onehot = np.zeros((2 * NA_COLS - 1, GRID_W * GRID_W), np.float32)
    onehot[dc.reshape(-1), np.arange(GRID_W * GRID_W)] = 1.0
    tcol = jnp.einsum("hab,bx->hax", rpb.astype(F32), onehot, precision=lax.Precision.HIGHEST)
    tcol = jnp.where(col_in[None, None], tcol.reshape(nh, 2 * NA_ROWS - 1, GRID_W, GRID_W), NEG_INF)
    masked = jnp.full((nh, GRID_W, GRID_W), NEG_INF, F32)
    meta = jnp.where(np.arange(GRID_W) < N_META, 0.0, masked)

    def half(dr):
        if dr is None:
            return masked
        return meta if dr == "meta" else tcol[:, dr + NA_ROWS - 1]

    tiles = jnp.stack([jnp.concatenate([half(l), half(r)], axis=-1) for l, r in specs], axis=1)
    return tiles * LOG2_E


def _na_kernel(q_ref, k0_ref, k1_ref, k2_ref, v0_ref, v1_ref, v2_ref, km_ref, vm_ref, tiles_ref,
               o_ref, bias_ref, k_scr, v_scr, *, scale, plans):
    hb = q_ref.shape[0]
    j = pl.program_id(2)
    last = pl.num_programs(2) - 1
    masked_tile = jnp.full((GRID_W, LANES), NEG_INF, F32)

    def build(plan):
        def body():
            for hh in range(hb):
                for qrl in range(NA_QROWS):
                    for t in range(NA_TILES_PER_ROW):
                        idx = plan[qrl][t]
                        tile = masked_tile if idx < 0 else tiles_ref[hh, idx]
                        bias_ref[hh, qrl * GRID_W:(qrl + 1) * GRID_W, t * LANES:(t + 1) * LANES] = tile
        return body

    pl.when(j == 0)(build(plans[0]))
    pl.when(j == 1)(build(plans[1]))
    pl.when(j == last)(build(plans[2]))

    meta_pos = pl.multiple_of(
        jnp.where(j == last, 0, (NA_KBLKS * NA_QROWS - 1) * GRID_W), N_META)
    for hh in range(hb):
        for d, (kb_ref, vb_ref) in enumerate(((k0_ref, v0_ref), (k1_ref, v1_ref), (k2_ref, v2_ref))):
            k_scr[hh, d * NA_QBLK:(d + 1) * NA_QBLK, :] = kb_ref[hh]
            v_scr[hh, d * NA_QBLK:(d + 1) * NA_QBLK, :] = vb_ref[hh]
        k_scr[hh, pl.ds(meta_pos, N_META), :] = km_ref[hh]
        v_scr[hh, pl.ds(meta_pos, N_META), :] = vm_ref[hh]
        q = q_ref[hh]
        k = k_scr[hh]
        v = v_scr[hh]
        s = _qkt(q, k) * (scale * LOG2_E) + bias_ref[hh]
        o_ref[:, hh * LANES:(hh + 1) * LANES] = _softmax_pv(s, v)


def _na_attention(proj, km, vm, tiles, plans, batch, rows):
    nh = km.shape[0]
    hb = NA_HEADS_PER_STEP
    nblk = rows // NA_QROWS
    tokens = proj.shape[1]
    assert nblk >= NA_KBLKS and nh % hb == 0
    hblks = nh // hb

    def kv_map(base, d):
        def f(h, b, j):
            ws = jnp.clip(j - 1, 0, nblk - NA_KBLKS)
            return (base + h, b * nblk + ws + d, 0)
        return f

    qspec = pl.BlockSpec((hb, NA_QBLK, LANES), lambda h, b, j: (h, b * nblk + j, 0))
    kspecs = [pl.BlockSpec((hb, NA_QBLK, LANES), kv_map(hblks, d)) for d in range(NA_KBLKS)]
    vspecs = [pl.BlockSpec((hb, NA_QBLK, LANES), kv_map(2 * hblks, d)) for d in range(NA_KBLKS)]
    mspec = pl.BlockSpec((hb, N_META, LANES), lambda h, b, j: (h, 0, 0))
    tspec = pl.BlockSpec((hb,) + tiles.shape[1:], lambda h, b, j: (h, 0, 0, 0))
    return pl.pallas_call(
        functools.partial(_na_kernel, scale=HEAD_DIM ** -0.5, plans=plans),
        grid=(hblks, batch, nblk),
        in_specs=[qspec] + kspecs + vspecs + [mspec, mspec, tspec],
        out_specs=pl.BlockSpec((NA_QBLK, hb * LANES), lambda h, b, j: (b * nblk + j, h)),
        out_shape=jax.ShapeDtypeStruct((tokens, nh * LANES), BF16),
        scratch_shapes=[pltpu.VMEM((hb, NA_QBLK, NA_KBLKS * NA_QBLK), F32),
                        pltpu.VMEM((hb, NA_KBLKS * NA_QBLK, LANES), BF16),
                        pltpu.VMEM((hb, NA_KBLKS * NA_QBLK, LANES), BF16)],
        compiler_params=_cparams(("arbitrary", "arbitrary", "arbitrary"), ATTN_VMEM_MIB["na"]),
        name="na_attn",
    )(proj, proj, proj, proj, proj, proj, proj, km, vm, tiles)


def _swa_kernel(q_ref, k0_ref, k1_ref, k2_ref, v0_ref, v1_ref, v2_ref, km_ref, vm_ref, bias_ref,
                o_ref, *, scale, group):
    hkb = k0_ref.shape[0]
    blk = q_ref.shape[1]
    for kh in range(hkb):
        heads = slice(kh * group, (kh + 1) * group)
        q = q_ref[heads].reshape(group * blk, LANES)
        k = jnp.concatenate([k0_ref[kh], k1_ref[kh], k2_ref[kh], km_ref[kh]], axis=0)
        v = jnp.concatenate([v0_ref[kh], v1_ref[kh], v2_ref[kh], vm_ref[kh]], axis=0)
        s = _qkt(q, k) * (scale * LOG2_E) + bias_ref[heads].reshape(group * blk, 3 * blk + LANES)
        o = _softmax_pv(s, v)
        for g in range(group):
            col = (kh * group + g) * LANES
            o_ref[:, col:col + LANES] = o[g * blk:(g + 1) * blk]


def _swa_attention(proj, km, vm, bias, batch, seq, q_base, k_base, v_base):
    hkv = km.shape[0]
    hq = bias.shape[1]
    group = hq // hkv
    hkb = SWA_KV_HEADS_PER_STEP
    blk = SWA_BLOCK
    nb = seq // blk
    tokens = proj.shape[1]
    assert hkv % hkb == 0 and k_base % hkb == 0 and v_base % hkb == 0 and q_base % (hkb * group) == 0

    def bias_map(hk, b, i):
        variant = jnp.where(i == 0, 1, 0) + jnp.where(i == nb - 1, 2, 0)
        return (variant, hk, 0, 0)

    def kv_map(base, d):
        def f(hk, b, i):
            return (base // hkb + hk, b * nb + jnp.clip(i - 1 + d, 0, nb - 1), 0)
        return f

    kspecs = [pl.BlockSpec((hkb, blk, LANES), kv_map(k_base, d)) for d in range(3)]
    vspecs = [pl.BlockSpec((hkb, blk, LANES), kv_map(v_base, d)) for d in range(3)]
    mspec = pl.BlockSpec((hkb, LANES, LANES), lambda hk, b, i: (hk, 0, 0))
    return pl.pallas_call(
        functools.partial(_swa_kernel, scale=HEAD_DIM ** -0.5, group=group),
        grid=(hkv // hkb, batch, nb),
        in_specs=[pl.BlockSpec((hkb * group, blk, LANES),
                               lambda hk, b, i: (q_base // (hkb * group) + hk, b * nb + i, 0))]
                 + kspecs + vspecs + [mspec, mspec,
                  pl.BlockSpec((None, hkb * group, blk, 3 * blk + LANES), bias_map)],
        out_specs=pl.BlockSpec((blk, hkb * group * LANES), lambda hk, b, i: (b * nb + i, hk)),
        out_shape=jax.ShapeDtypeStruct((tokens, hq * LANES), BF16),
        compiler_params=_cparams(("parallel", "parallel", "parallel"), ATTN_VMEM_MIB["swa"]),
        name="swa_attn",
    )(proj, proj, proj, proj, proj, proj, proj, km, vm, bias)


def _t5_bucket(rel):
    nb = T5_BUCKETS // 2
    max_exact = nb // 2
    ret = np.where(rel > 0, nb, 0)
    n = np.abs(rel)
    large = max_exact + (np.log(np.maximum(n, 1) / max_exact)
                         / math.log(T5_MAX_DIST / max_exact) * (nb - max_exact)).astype(np.int64)
    large = np.minimum(large, nb - 1)
    return ret + np.where(n < max_exact, n, large)


def _select_rows(table, idx):
    flat = idx.reshape(-1)
    onehot = np.zeros((table.shape[0], flat.size), np.float32)
    onehot[flat, np.arange(flat.size)] = 1.0
    out = jnp.einsum("bh,bx->xh", table.astype(F32), onehot, precision=lax.Precision.HIGHEST)
    return out.reshape(idx.shape + (table.shape[1],))


def _swa_tables(t5_bias, sink):
    blk = SWA_BLOCK
    rel = np.arange(-(2 * blk - 1), 2 * blk)
    by_rel = _select_rows(t5_bias, _t5_bucket(rel)).T
    by_rel = jnp.where((np.abs(rel) <= SWA_WINDOW)[None], by_rel, NEG_INF)
    bias_w = jnp.stack([by_rel[:, blk - 1 - q:blk - 1 - q + 3 * blk] for q in range(blk)], axis=1)
    jj = np.arange(3 * blk)
    edge = np.stack([np.zeros_like(jj, bool), jj < blk, jj >= 2 * blk, (jj < blk) | (jj >= 2 * blk)])
    bias_w = jnp.where(edge[:, None, None, :], NEG_INF, bias_w[None])
    assert N_META + blk - (N_META - 1) > T5_MAX_DIST
    rel_m = np.arange(N_META)[None, :] - (N_META + np.arange(2 * blk))[:, None]
    bias_m = _select_rows(t5_bias, _t5_bucket(rel_m))
    bias_m = bias_m.reshape(2, blk, N_META, -1).transpose(0, 3, 1, 2)
    bias_m = jnp.stack([bias_m[1], bias_m[0], bias_m[1], bias_m[0]])
    hq = bias_m.shape[1]
    sink_col = jnp.broadcast_to(sink.astype(F32)[None, :, None, None], (4, hq, blk, 1))
    pad = jnp.full((4, hq, blk, LANES - N_META - 1), NEG_INF, F32)
    return jnp.concatenate([bias_w, bias_m, sink_col, pad], axis=-1) * LOG2_E


def _trunk(x, meta_kv, tabs, params, later_weights):
    batch, seq, d = x.shape
    (g_attn, w_in, head_gain, head_flag, g_mlp) = params
    km_na, vm_na, km_sw, vm_sw = meta_kv
    na_tiles, na_plans, swa_bias = tabs
    nh_na = km_na.shape[0]
    hq_sw = swa_bias.shape[1]
    hkv_sw = km_sw.shape[0]
    x2 = x.reshape(batch * seq, d)
    casts = () if later_weights[0].dtype == BF16 else later_weights
    proj, *cast_out = _inproj(x2, g_attn, w_in, head_gain, head_flag, TILES["inproj"]["bm"], casts)
    w_out, w_up, w_down = later_weights = tuple(cast_out) if cast_out else later_weights
    rows = seq // GRID_W
    assert seq % (GRID_W * NA_QROWS) == 0 and rows >= NA_ROWS
    assert seq % SWA_BLOCK == 0
    na_o = _na_attention(proj, km_na, vm_na, na_tiles, na_plans, batch, rows)
    q_base = 3 * nh_na
    k_base = q_base + hq_sw
    v_base = k_base + hkv_sw
    sw_o = _swa_attention(proj, km_sw, vm_sw, swa_bias, batch, seq, q_base, k_base, v_base)
    x1, x1g, x1ss = _outproj(na_o, sw_o, w_out, x2, g_mlp)
    u = _mlp_up(x1g, x1ss, w_up)
    y = _mlp_down(u, w_down, x1)
    return y.reshape(batch, seq, d), later_weights


def kernel(x_prompt, x_sample, meta_tokens, t5_bias, norm_attn, w_in, q_norm_na, k_norm_na, na_rpb,
           q_norm_swa, k_norm_swa, swa_sink, w_out, norm_mlp, w_up, w_down):
    depth = w_in.shape[0]
    assert depth == 1
    nh_na = na_rpb.shape[1]
    hq_sw = swa_sink.shape[1]
    in_width = w_in.shape[2]
    hkv_sw = (in_width // HEAD_DIM - 3 * nh_na - hq_sw) // 2

    ones = jnp.ones((HEAD_DIM,), F32)

    def rep(g, n):
        return jnp.broadcast_to(g.astype(F32)[None], (n, HEAD_DIM))

    head_gain = jnp.concatenate([
        rep(q_norm_na[0], nh_na), rep(k_norm_na[0], nh_na), rep(ones, nh_na),
        rep(q_norm_swa[0], hq_sw), rep(k_norm_swa[0], hkv_sw), rep(ones, hkv_sw)])[:, None, :]
    flag = np.concatenate([np.ones(2 * nh_na), np.zeros(nh_na), np.ones(hq_sw + hkv_sw),
                           np.zeros(hkv_sw)]).astype(np.float32)
    head_flag = jnp.asarray(np.broadcast_to(flag[:, None, None], (flag.shape[0], 1, HEAD_DIM)))

    w_in_b = w_in[0].astype(BF16)
    g_attn = norm_attn[0].astype(F32)[None]
    g_mlp = norm_mlp[0].astype(F32)[None]

    mproj, = _inproj(meta_tokens.astype(F32), g_attn, w_in_b, head_gain, head_flag, N_META)
    k0 = 3 * nh_na + hq_sw
    mpad = jnp.pad(mproj[k0:k0 + 2 * hkv_sw], ((0, 0), (0, LANES - N_META), (0, 0)))
    meta_kv = (mproj[nh_na:2 * nh_na], mproj[2 * nh_na:3 * nh_na], mpad[:hkv_sw], mpad[hkv_sw:])

    specs, plans = _na_plan()
    tabs = (_na_tiles(na_rpb[0], specs), plans, _swa_tables(t5_bias, swa_sink[0]))
    params = (g_attn, w_in_b, head_gain, head_flag, g_mlp)
    weights_f32 = (w_out[0].astype(F32), w_up[0].astype(F32), w_down[0].astype(F32))
    y_sample, weights_bf16 = _trunk(x_sample, meta_kv, tabs, params, weights_f32)
    y_prompt, _ = _trunk(x_prompt, meta_kv, tabs, params, weights_bf16)
    return (y_prompt, y_sample)
```

```python
import functools
import math

import jax
import jax.numpy as jnp
import numpy as np
from jax import lax
from jax.experimental import pallas as pl
from jax.experimental.pallas import tpu as pltpu

F32 = jnp.float32
BF16 = jnp.bfloat16

HEAD_DIM = 128
N_META = 16
GRID_W = 64
NA_ROWS = 8
NA_COLS = 16
SWA_WINDOW = 128
SWA_BLOCK = 128
T5_BUCKETS = 32
T5_MAX_DIST = 128
NORM_EPS = 1e-6
NEG_INF = -1e30
LOG2_E = math.log2(math.e)

LANES = 128
NA_QROWS = 4
NA_QBLK = NA_QROWS * GRID_W
NA_KBLKS = 3
NA_HEADS_PER_STEP = 8
SWA_KV_HEADS_PER_STEP = 4
NA_TILES_PER_ROW = NA_KBLKS * NA_QBLK // LANES
MIB = 1024 * 1024
V7X_VMEM_MIB = 64
BF16_SUBLANES = 16

TILES = {
    "meta_inproj": dict(bn=512, vmem=40),
    "inproj": dict(bm=512, bn=1024, vmem=60),
    "outproj": dict(bm=512, bn=2048, vmem=56),
    "mlp_up": dict(bm=1024, bn=1024, vmem=52),
    "mlp_down": dict(bm=1024, bn=1024, bk=4096, vmem=60),
}
ATTN_VMEM_MIB = dict(na=48, swa=32)

assert 2 * GRID_W == LANES
assert all(t["vmem"] < V7X_VMEM_MIB for t in TILES.values())


def _cparams(sem, vmem_mib):
    return pltpu.CompilerParams(dimension_semantics=sem, vmem_limit_bytes=vmem_mib * MIB)


def _rms_rows(x, gain):
    ms = jnp.mean(x * x, axis=-1, keepdims=True)
    return x * lax.rsqrt(ms + NORM_EPS) * gain


def _head_norm_store(res, hg_ref, hf_ref, o_ref):
    for c in range(o_ref.shape[0]):
        blk = res[:, c * LANES:(c + 1) * LANES]
        r = lax.rsqrt(jnp.mean(blk * blk, axis=-1, keepdims=True) + NORM_EPS)
        f = hf_ref[c]
        o_ref[c] = (blk * (f * r + (1.0 - f)) * hg_ref[c]).astype(BF16)


def _meta_inproj_kernel(x_ref, g_ref, w_ref, hg_ref, hf_ref, o_ref, wb_ref):
    wb = w_ref[...].astype(BF16)
    wb_ref[...] = wb
    xn = _rms_rows(x_ref[...], g_ref[...]).astype(BF16)
    _head_norm_store(jnp.dot(xn, wb, preferred_element_type=F32), hg_ref, hf_ref, o_ref)


def _meta_inproj(x_meta, gain, w, head_gain, head_flag):
    m, d = x_meta.shape
    n = w.shape[1]
    bn, vmem = TILES["meta_inproj"]["bn"], TILES["meta_inproj"]["vmem"]
    cpb = bn // LANES
    return pl.pallas_call(
        _meta_inproj_kernel,
        grid=(n // bn,),
        in_specs=[
            pl.BlockSpec((m, d), lambda j: (0, 0)),
            pl.BlockSpec((1, d), lambda j: (0, 0)),
            pl.BlockSpec((d, bn), lambda j: (0, j)),
            pl.BlockSpec((cpb, 1, LANES), lambda j: (j, 0, 0)),
            pl.BlockSpec((cpb, 1, LANES), lambda j: (j, 0, 0)),
        ],
        out_specs=[pl.BlockSpec((cpb, m, LANES), lambda j: (j, 0, 0)),
                   pl.BlockSpec((d, bn), lambda j: (0, j))],
        out_shape=[jax.ShapeDtypeStruct((n // LANES, m, LANES), BF16),
                   jax.ShapeDtypeStruct((d, n), BF16)],
        compiler_params=_cparams(("parallel",), vmem),
        name="meta_inproj",
    )(x_meta, gain, w, head_gain, head_flag)


def _inproj_kernel(x_ref, g_ref, w_ref, hg_ref, hf_ref, *refs, n_cast, nj, ntiles):
    cast_in, o_ref, cast_out = refs[:n_cast], refs[n_cast], refs[n_cast + 1:2 * n_cast + 1]
    xn_ref, raw_even, raw_odd = refs[-3:]
    s = pl.program_id(0)
    for src_ref, dst_ref in zip(cast_in, cast_out):
        dst_ref[...] = src_ref[...].astype(BF16)

    @pl.when(s == 0)
    def _():
        raw_odd[...] = jnp.zeros(raw_odd.shape, F32)

    @pl.when((s % nj == 0) & (s < ntiles))
    def _():
        xn_ref[...] = _rms_rows(x_ref[...], g_ref[...]).astype(BF16)

    def step(cur_ref, prev_ref):
        def body():
            cur_ref[...] = jnp.dot(xn_ref[...], w_ref[...], preferred_element_type=F32)
            _head_norm_store(prev_ref, hg_ref, hf_ref, o_ref)
        return body

    pl.when(s % 2 == 0)(step(raw_even, raw_odd))
    pl.when(s % 2 == 1)(step(raw_odd, raw_even))


def _inproj(x2, gain, w, head_gain, head_flag, bm, casts=()):
    m, d = x2.shape
    n = w.shape[1]
    bn, vmem = TILES["inproj"]["bn"], TILES["inproj"]["vmem"]
    cpb = bn // LANES
    ni, nj = m // bm, n // bn
    ntiles = ni * nj

    def cur(s):
        return jnp.minimum(s, ntiles - 1)

    def prev(s):
        return jnp.maximum(s - 1, 0)

    cast_specs, cast_shapes = [], []
    for wc in casts:
        rows, cols = wc.shape
        nchunks = 1 << (ntiles.bit_length() - 1)
        rpc = rows // nchunks
        assert rows % nchunks == 0 and rpc % BF16_SUBLANES == 0
        cast_specs.append(pl.BlockSpec(
            (rpc, cols), lambda s, nchunks=nchunks: (jnp.minimum(s, nchunks - 1), 0)))
        cast_shapes.append(jax.ShapeDtypeStruct((rows, cols), BF16))
    outs = pl.pallas_call(
        functools.partial(_inproj_kernel, n_cast=len(casts), nj=nj, ntiles=ntiles),
        grid=(ntiles + 1,),
        in_specs=[
            pl.BlockSpec((bm, d), lambda s: (cur(s) // nj, 0)),
            pl.BlockSpec((1, d), lambda s: (0, 0)),
            pl.BlockSpec((d, bn), lambda s: (0, cur(s) % nj)),
            pl.BlockSpec((cpb, 1, LANES), lambda s: (prev(s) % nj, 0, 0)),
            pl.BlockSpec((cpb, 1, LANES), lambda s: (prev(s) % nj, 0, 0)),
        ] + cast_specs,
        out_specs=[pl.BlockSpec((cpb, bm, LANES), lambda s: (prev(s) % nj, prev(s) // nj, 0))]
                  + cast_specs,
        out_shape=[jax.ShapeDtypeStruct((n // LANES, m, LANES), BF16)] + cast_shapes,
        scratch_shapes=[pltpu.VMEM((bm, d), BF16), pltpu.VMEM((bm, bn), F32),
                        pltpu.VMEM((bm, bn), F32)],
        compiler_params=_cparams(("arbitrary",), vmem),
        name="inproj",
    )(x2, gain, w, head_gain, head_flag, *casts)
    return outs


def _mlp_up_kernel(a_ref, ss_ref, w_ref, o_ref, *, d):
    ss = ss_ref[0][:, :1]
    for c in range(1, ss_ref.shape[0]):
        ss = ss + ss_ref[c][:, :1]
    r = lax.rsqrt(ss * (1.0 / d) + NORM_EPS)
    u = jnp.dot(a_ref[...], w_ref[...], preferred_element_type=F32) * r
    u = jnp.maximum(u, 0.0)
    o_ref[...] = (u * u).astype(BF16)


def _mlp_up(a, ss, w):
    m, d = a.shape
    n = w.shape[1]
    ncol = ss.shape[0]
    bm, bn, vmem = (TILES["mlp_up"][k] for k in ("bm", "bn", "vmem"))
    return pl.pallas_call(
        functools.partial(_mlp_up_kernel, d=d),
        grid=(m // bm, n // bn),
        in_specs=[
            pl.BlockSpec((bm, d), lambda i, j: (i, 0)),
            pl.BlockSpec((ncol, bm, LANES), lambda i, j: (0, i, 0)),
            pl.BlockSpec((d, bn), lambda i, j: (0, j)),
        ],
        out_specs=pl.BlockSpec((bm, bn), lambda i, j: (i, j)),
        out_shape=jax.ShapeDtypeStruct((m, n), BF16),
        compiler_params=_cparams(("parallel", "parallel"), vmem),
        name="mlp_up",
    )(a, ss, w)


def _mlp_down_kernel(a_ref, w_ref, r_ref, o_ref):
    @pl.when(pl.program_id(2) == 0)
    def _():
        o_ref[...] = r_ref[...]

    o_ref[...] += jnp.dot(a_ref[...], w_ref[...], preferred_element_type=F32)


def _mlp_down(a, w, resid):
    m, kd = a.shape
    n = w.shape[1]
    bm, bn, bk, vmem = (TILES["mlp_down"][k] for k in ("bm", "bn", "bk", "vmem"))
    return pl.pallas_call(
        _mlp_down_kernel,
        grid=(m // bm, n // bn, kd // bk),
        in_specs=[
            pl.BlockSpec((bm, bk), lambda i, j, k: (i, k)),
            pl.BlockSpec((bk, bn), lambda i, j, k: (k, j)),
            pl.BlockSpec((bm, bn), lambda i, j, k: (i, j)),
        ],
        out_specs=pl.BlockSpec((bm, bn), lambda i, j, k: (i, j)),
        out_shape=jax.ShapeDtypeStruct((m, n), F32),
        compiler_params=_cparams(("parallel", "parallel", "arbitrary"), vmem),
        name="mlp_down",
    )(a, w, resid)


def _outproj_kernel(na_ref, sw_ref, wt_ref, wb_ref, r_ref, g_ref, o_ref, ob_ref, ss_ref):
    acc = jnp.dot(na_ref[...], wt_ref[...], preferred_element_type=F32)
    acc = acc + jnp.dot(sw_ref[...], wb_ref[...], preferred_element_type=F32)
    x1 = r_ref[...] + acc
    o_ref[...] = x1
    ob_ref[...] = (x1 * g_ref[...]).astype(BF16)
    ss_ref[...] = jnp.broadcast_to(jnp.sum(x1 * x1, axis=-1, keepdims=True), ss_ref.shape)


def _outproj(na_o, sw_o, w, resid, gain):
    m, kh = na_o.shape
    n = w.shape[1]
    bm, bn, vmem = (TILES["outproj"][k] for k in ("bm", "bn", "vmem"))
    return pl.pallas_call(
        _outproj_kernel,
        grid=(n // bn, m // bm),
        in_specs=[
            pl.BlockSpec((bm, kh), lambda j, i: (i, 0)),
            pl.BlockSpec((bm, kh), lambda j, i: (i, 0)),
            pl.BlockSpec((kh, bn), lambda j, i: (0, j), pipeline_mode=pl.Buffered(1)),
            pl.BlockSpec((kh, bn), lambda j, i: (1, j), pipeline_mode=pl.Buffered(1)),
            pl.BlockSpec((bm, bn), lambda j, i: (i, j)),
            pl.BlockSpec((1, bn), lambda j, i: (0, j)),
        ],
        out_specs=[pl.BlockSpec((bm, bn), lambda j, i: (i, j)),
                   pl.BlockSpec((bm, bn), lambda j, i: (i, j)),
                   pl.BlockSpec((None, bm, LANES), lambda j, i: (j, i, 0))],
        out_shape=[jax.ShapeDtypeStruct((m, n), F32),
                   jax.ShapeDtypeStruct((m, n), BF16),
                   jax.ShapeDtypeStruct((n // bn, m, LANES), F32)],
        compiler_params=_cparams(("parallel", "parallel"), vmem),
        name="outproj",
    )(na_o, sw_o, w, w, resid, gain)


def _qkt(q, k):
    return lax.dot_general(q, k, (((1,), (1,)), ((), ())), preferred_element_type=F32)


def _softmax_pv(s, v):
    p = jnp.exp2(s - jnp.max(s, axis=-1, keepdims=True)).astype(BF16)
    o = jnp.dot(p, jnp.concatenate([v, jnp.ones_like(v)], axis=1), preferred_element_type=F32)
    return (o[:, :LANES] / o[:, LANES:]).astype(BF16)


def _na_plan():
    vblk = 6
    vrows = vblk * NA_QROWS
    wrows = NA_KBLKS * NA_QROWS
    specs, plans = [], []
    for j in (0, 2, vblk - 1):
        ws = int(np.clip(j - 1, 0, vblk - NA_KBLKS))
        meta_row = 0 if j == vblk - 1 else wrows - 1
        plan = []
        for qrl in range(NA_QROWS):
            qr = NA_QROWS * j + qrl
            rs = int(np.clip(qr - NA_ROWS // 2, 0, vrows - NA_ROWS))
            row = []
            for t in range(NA_TILES_PER_ROW):
                halves = []
                for krl in (2 * t, 2 * t + 1):
                    kr = NA_QROWS * ws + krl
                    half = kr - qr if rs <= kr < rs + NA_ROWS else None
                    if krl == meta_row:
                        assert half is None
                        half = "meta"
                    halves.append(half)
                halves = tuple(halves)
                if halves == (None, None):
                    row.append(-1)
                else:
                    if halves not in specs:
                        specs.append(halves)
                    row.append(specs.index(halves))
            plan.append(row)
        plans.append(plan)
    return specs, plans


def _na_tiles(rpb, specs):
    nh = rpb.shape[0]
    c = np.arange(GRID_W)
    cs = np.clip(c - NA_COLS // 2, 0, GRID_W - NA_COLS)
    col_in = (c[None, :] >= cs[:, None]) & (c[None, :] < cs[:, None] + NA_COLS)
    dc = np.clip(c[None, :] - c[:, None], -(NA_COLS - 1), NA_COLS - 1) + (NA_COLS - 1)
    onehot = np.zeros((2 * NA_COLS - 1, GRID_W * GRID_W), np.float32)
    onehot[dc.reshape(-1), np.arange(GRID_W * GRID_W)] = 1.0
    tcol = jnp.einsum("hab,bx->hax", rpb.astype(F32), onehot, precision=lax.Precision.HIGHEST)
    tcol = jnp.where(col_in[None, None], tcol.reshape(nh, 2 * NA_ROWS - 1, GRID_W, GRID_W), NEG_INF)
    masked = jnp.full((nh, GRID_W, GRID_W), NEG_INF, F32)
    meta = jnp.where(np.arange(GRID_W) < N_META, 0.0, masked)

    def half(dr):
        if dr is None:
            return masked
        return meta if dr == "meta" else tcol[:, dr + NA_ROWS - 1]

    tiles = jnp.stack([jnp.concatenate([half(l), half(r)], axis=-1) for l, r in specs], axis=1)
    return tiles * LOG2_E


def _na_kernel(q_ref, k0_ref, k1_ref, k2_ref, v0_ref, v1_ref, v2_ref, km_ref, vm_ref, tiles_ref,
               o_ref, bias_ref, k_scr, v_scr, *, scale, plans):
    hb = q_ref.shape[0]
    j = pl.program_id(2)
    last = pl.num_programs(2) - 1
    masked_tile = jnp.full((GRID_W, LANES), NEG_INF, F32)

    def build(plan):
        def body():
            for hh in range(hb):
                for qrl in range(NA_QROWS):
                    for t in range(NA_TILES_PER_ROW):
                        idx = plan[qrl][t]
                        tile = masked_tile if idx < 0 else tiles_ref[hh, idx]
                        bias_ref[hh, qrl * GRID_W:(qrl + 1) * GRID_W, t * LANES:(t + 1) * LANES] = tile
        return body

    pl.when(j == 0)(build(plans[0]))
    pl.when(j == 1)(build(plans[1]))
    pl.when(j == last)(build(plans[2]))

    meta_pos = pl.multiple_of(
        jnp.where(j == last, 0, (NA_KBLKS * NA_QROWS - 1) * GRID_W), N_META)
    for hh in range(hb):
        for d, (kb_ref, vb_ref) in enumerate(((k0_ref, v0_ref), (k1_ref, v1_ref), (k2_ref, v2_ref))):
            k_scr[hh, d * NA_QBLK:(d + 1) * NA_QBLK, :] = kb_ref[hh]
            v_scr[hh, d * NA_QBLK:(d + 1) * NA_QBLK, :] = vb_ref[hh]
        k_scr[hh, pl.ds(meta_pos, N_META), :] = km_ref[hh]
        v_scr[hh, pl.ds(meta_pos, N_META), :] = vm_ref[hh]
        q = q_ref[hh]
        k = k_scr[hh]
        v = v_scr[hh]
        s = _qkt(q, k) * (scale * LOG2_E) + bias_ref[hh]
        o_ref[:, hh * LANES:(hh + 1) * LANES] = _softmax_pv(s, v)


def _na_attention(proj, km, vm, tiles, plans, batch, rows):
    nh = km.shape[0]
    hb = NA_HEADS_PER_STEP
    nblk = rows // NA_QROWS
    tokens = proj.shape[1]
    assert nblk >= NA_KBLKS and nh % hb == 0
    hblks = nh // hb

    def kv_map(base, d):
        def f(h, b, j):
            ws = jnp.clip(j - 1, 0, nblk - NA_KBLKS)
            return (base + h, b * nblk + ws + d, 0)
        return f

    qspec = pl.BlockSpec((hb, NA_QBLK, LANES), lambda h, b, j: (h, b * nblk + j, 0))
    kspecs = [pl.BlockSpec((hb, NA_QBLK, LANES), kv_map(hblks, d)) for d in range(NA_KBLKS)]
    vspecs = [pl.BlockSpec((hb, NA_QBLK, LANES), kv_map(2 * hblks, d)) for d in range(NA_KBLKS)]
    mspec = pl.BlockSpec((hb, N_META, LANES), lambda h, b, j: (h, 0, 0))
    tspec = pl.BlockSpec((hb,) + tiles.shape[1:], lambda h, b, j: (h, 0, 0, 0))
    return pl.pallas_call(
        functools.partial(_na_kernel, scale=HEAD_DIM ** -0.5, plans=plans),
        grid=(hblks, batch, nblk),
        in_specs=[qspec] + kspecs + vspecs + [mspec, mspec, tspec],
        out_specs=pl.BlockSpec((NA_QBLK, hb * LANES), lambda h, b, j: (b * nblk + j, h)),
        out_shape=jax.ShapeDtypeStruct((tokens, nh * LANES), BF16),
        scratch_shapes=[pltpu.VMEM((hb, NA_QBLK, NA_KBLKS * NA_QBLK), F32),
                        pltpu.VMEM((hb, NA_KBLKS * NA_QBLK, LANES), BF16),
                        pltpu.VMEM((hb, NA_KBLKS * NA_QBLK, LANES), BF16)],
        compiler_params=_cparams(("arbitrary", "arbitrary", "arbitrary"), ATTN_VMEM_MIB["na"]),
        name="na_attn",
    )(proj, proj, proj, proj, proj, proj, proj, km, vm, tiles)


def _swa_kernel(q_ref, k0_ref, k1_ref, k2_ref, v0_ref, v1_ref, v2_ref, km_ref, vm_ref, bias_ref,
                o_ref, *, scale, group):
    hkb = k0_ref.shape[0]
    blk = q_ref.shape[1]
    for kh in range(hkb):
        heads = slice(kh * group, (kh + 1) * group)
        q = q_ref[heads].reshape(group * blk, LANES)
        k = jnp.concatenate([k0_ref[kh], k1_ref[kh], k2_ref[kh], km_ref[kh]], axis=0)
        v = jnp.concatenate([v0_ref[kh], v1_ref[kh], v2_ref[kh], vm_ref[kh]], axis=0)
        s = _qkt(q, k) * (scale * LOG2_E) + bias_ref[heads].reshape(group * blk, 3 * blk + LANES)
        o = _softmax_pv(s, v)
        for g in range(group):
            col = (kh * group + g) * LANES
            o_ref[:, col:col + LANES] = o[g * blk:(g + 1) * blk]


def _swa_attention(proj, km, vm, bias, batch, seq, q_base, k_base, v_base):
    hkv = km.shape[0]
    hq = bias.shape[1]
    group = hq // hkv
    hkb = SWA_KV_HEADS_PER_STEP
    blk = SWA_BLOCK
    nb = seq // blk
    tokens = proj.shape[1]
    assert hkv % hkb == 0 and k_base % hkb == 0 and v_base % hkb == 0 and q_base % (hkb * group) == 0

    def bias_map(hk, b, i):
        variant = jnp.where(i == 0, 1, 0) + jnp.where(i == nb - 1, 2, 0)
        return (variant, hk, 0, 0)

    def kv_map(base, d):
        def f(hk, b, i):
            return (base // hkb + hk, b * nb + jnp.clip(i - 1 + d, 0, nb - 1), 0)
        return f

    kspecs = [pl.BlockSpec((hkb, blk, LANES), kv_map(k_base, d)) for d in range(3)]
    vspecs = [pl.BlockSpec((hkb, blk, LANES), kv_map(v_base, d)) for d in range(3)]
    mspec = pl.BlockSpec((hkb, LANES, LANES), lambda hk, b, i: (hk, 0, 0))
    return pl.pallas_call(
        functools.partial(_swa_kernel, scale=HEAD_DIM ** -0.5, group=group),
        grid=(hkv // hkb, batch, nb),
        in_specs=[pl.BlockSpec((hkb * group, blk, LANES),
                               lambda hk, b, i: (q_base // (hkb * group) + hk, b * nb + i, 0))]
                 + kspecs + vspecs + [mspec, mspec,
                  pl.BlockSpec((None, hkb * group, blk, 3 * blk + LANES), bias_map)],
        out_specs=pl.BlockSpec((blk, hkb * group * LANES), lambda hk, b, i: (b * nb + i, hk)),
        out_shape=jax.ShapeDtypeStruct((tokens, hq * LANES), BF16),
        compiler_params=_cparams(("parallel", "parallel", "parallel"), ATTN_VMEM_MIB["swa"]),
        name="swa_attn",
    )(proj, proj, proj, proj, proj, proj, proj, km, vm, bias)


def _t5_bucket(rel):
    nb = T5_BUCKETS // 2
    max_exact = nb // 2
    ret = np.where(rel > 0, nb, 0)
    n = np.abs(rel)
    large = max_exact + (np.log(np.maximum(n, 1) / max_exact)
                         / math.log(T5_MAX_DIST / max_exact) * (nb - max_exact)).astype(np.int64)
    large = np.minimum(large, nb - 1)
    return ret + np.where(n < max_exact, n, large)


def _select_rows(table, idx):
    flat = idx.reshape(-1)
    onehot = np.zeros((table.shape[0], flat.size), np.float32)
    onehot[flat, np.arange(flat.size)] = 1.0
    out = jnp.einsum("bh,bx->xh", table.astype(F32), onehot, precision=lax.Precision.HIGHEST)
    return out.reshape(idx.shape + (table.shape[1],))


def _swa_tables(t5_bias, sink):
    blk = SWA_BLOCK
    rel = np.arange(-(2 * blk - 1), 2 * blk)
    by_rel = _select_rows(t5_bias, _t5_bucket(rel)).T
    by_rel = jnp.where((np.abs(rel) <= SWA_WINDOW)[None], by_rel, NEG_INF)
    bias_w = jnp.stack([by_rel[:, blk - 1 - q:blk - 1 - q + 3 * blk] for q in range(blk)], axis=1)
    jj = np.arange(3 * blk)
    edge = np.stack([np.zeros_like(jj, bool), jj < blk, jj >= 2 * blk, (jj < blk) | (jj >= 2 * blk)])
    bias_w = jnp.where(edge[:, None, None, :], NEG_INF, bias_w[None])
    assert N_META + blk - (N_META - 1) > T5_MAX_DIST
    rel_m = np.arange(N_META)[None, :] - (N_META + np.arange(2 * blk))[:, None]
    bias_m = _select_rows(t5_bias, _t5_bucket(rel_m))
    bias_m = bias_m.reshape(2, blk, N_META, -1).transpose(0, 3, 1, 2)
    bias_m = jnp.stack([bias_m[1], bias_m[0], bias_m[1], bias_m[0]])
    hq = bias_m.shape[1]
    sink_col = jnp.broadcast_to(sink.astype(F32)[None, :, None, None], (4, hq, blk, 1))
    pad = jnp.full((4, hq, blk, LANES - N_META - 1), NEG_INF, F32)
    return jnp.concatenate([bias_w, bias_m, sink_col, pad], axis=-1) * LOG2_E


def _trunk(x, meta_kv, tabs, params, later_weights):
    batch, seq, d = x.shape
    (g_attn, w_in, head_gain, head_flag, g_mlp) = params
    km_na, vm_na, km_sw, vm_sw = meta_kv
    na_tiles, na_plans, swa_bias = tabs
    nh_na = km_na.shape[0]
    hq_sw = swa_bias.shape[1]
    hkv_sw = km_sw.shape[0]
    x2 = x.reshape(batch * seq, d)
    casts = () if later_weights[0].dtype == BF16 else later_weights
    proj, *cast_out = _inproj(x2, g_attn, w_in, head_gain, head_flag, TILES["inproj"]["bm"], casts)
    w_out, w_up, w_down = later_weights = tuple(cast_out) if cast_out else later_weights
    rows = seq // GRID_W
    assert seq % (GRID_W * NA_QROWS) == 0 and rows >= NA_ROWS
    assert seq % SWA_BLOCK == 0
    na_o = _na_attention(proj, km_na, vm_na, na_tiles, na_plans, batch, rows)
    q_base = 3 * nh_na
    k_base = q_base + hq_sw
    v_base = k_base + hkv_sw
    sw_o = _swa_attention(proj, km_sw, vm_sw, swa_bias, batch, seq, q_base, k_base, v_base)
    x1, x1g, x1ss = _outproj(na_o, sw_o, w_out, x2, g_mlp)
    u = _mlp_up(x1g, x1ss, w_up)
    y = _mlp_down(u, w_down, x1)
    return y.reshape(batch, seq, d), later_weights


def kernel(x_prompt, x_sample, meta_tokens, t5_bias, norm_attn, w_in, q_norm_na, k_norm_na, na_rpb,
           q_norm_swa, k_norm_swa, swa_sink, w_out, norm_mlp, w_up, w_down):
    depth = w_in.shape[0]
    assert depth == 1
    nh_na = na_rpb.shape[1]
    hq_sw = swa_sink.shape[1]
    in_width = w_in.shape[2]
    hkv_sw = (in_width // HEAD_DIM - 3 * nh_na - hq_sw) // 2

    ones = jnp.ones((HEAD_DIM,), F32)

    def rep(g, n):
        return jnp.broadcast_to(g.astype(F32)[None], (n, HEAD_DIM))

    head_gain = jnp.concatenate([
        rep(q_norm_na[0], nh_na), rep(k_norm_na[0], nh_na), rep(ones, nh_na),
        rep(q_norm_swa[0], hq_sw), rep(k_norm_swa[0], hkv_sw), rep(ones, hkv_sw)])[:, None, :]
    flag = np.concatenate([np.ones(2 * nh_na), np.zeros(nh_na), np.ones(hq_sw + hkv_sw),
                           np.zeros(hkv_sw)]).astype(np.float32)
    head_flag = jnp.asarray(np.broadcast_to(flag[:, None, None], (flag.shape[0], 1, HEAD_DIM)))

    g_attn = norm_attn[0].astype(F32)[None]
    g_mlp = norm_mlp[0].astype(F32)[None]

    mproj, w_in_b = _meta_inproj(meta_tokens.astype(F32), g_attn, w_in[0].astype(F32),
                                 head_gain, head_flag)
    k0 = 3 * nh_na + hq_sw
    mpad = jnp.pad(mproj[k0:k0 + 2 * hkv_sw], ((0, 0), (0, LANES - N_META), (0, 0)))
    meta_kv = (mproj[nh_na:2 * nh_na], mproj[2 * nh_na:3 * nh_na], mpad[:hkv_sw], mpad[hkv_sw:])

    specs, plans = _na_plan()
    tabs = (_na_tiles(na_rpb[0], specs), plans, _swa_tables(t5_bias, swa_sink[0]))
    params = (g_attn, w_in_b, head_gain, head_flag, g_mlp)
    weights_f32 = (w_out[0].astype(F32), w_up[0].astype(F32), w_down[0].astype(F32))
    y_sample, weights_bf16 = _trunk(x_sample, meta_kv, tabs, params, weights_f32)
    y_prompt, _ = _trunk(x_prompt, meta_kv, tabs, params, weights_bf16)
    return (y_prompt, y_sample)
```

```python
import functools
import math

import jax
import jax.numpy as jnp
import numpy as np
from jax import lax
from jax.experimental import pallas as pl
from jax.experimental.pallas import tpu as pltpu

F32 = jnp.float32
BF16 = jnp.bfloat16

HEAD_DIM = 128
N_META = 16
GRID_W = 64
NA_ROWS = 8
NA_COLS = 16
SWA_WINDOW = 128
SWA_BLOCK = 128
T5_BUCKETS = 32
T5_MAX_DIST = 128
NORM_EPS = 1e-6
NEG_INF = -1e30
LOG2_E = math.log2(math.e)

LANES = 128
NA_QROWS = 4
NA_QBLK = NA_QROWS * GRID_W
NA_KBLKS = 3
NA_HEADS_PER_STEP = 16
SWA_KV_HEADS_PER_STEP = 4
NA_TILES_PER_ROW = NA_KBLKS * NA_QBLK // LANES
MIB = 1024 * 1024
V7X_VMEM_MIB = 64
BF16_SUBLANES = 16

TILES = {
    "meta_inproj": dict(bn=512, vmem=40),
    "inproj": dict(bm=512, bn=1024, vmem=60),
    "outproj": dict(bm=512, bn=2048, vmem=56),
    "mlp_up": dict(bm=1024, bn=1024, vmem=52),
    "mlp_down": dict(bm=1024, bn=1024, bk=4096, vmem=60),
}
ATTN_VMEM_MIB = dict(na=56, swa=32)

assert 2 * GRID_W == LANES
assert all(t["vmem"] < V7X_VMEM_MIB for t in TILES.values())


def _cparams(sem, vmem_mib):
    return pltpu.CompilerParams(dimension_semantics=sem, vmem_limit_bytes=vmem_mib * MIB)


def _rms_rows(x, gain):
    ms = jnp.mean(x * x, axis=-1, keepdims=True)
    return x * lax.rsqrt(ms + NORM_EPS) * gain


def _head_norm_store(res, hg_ref, hf_ref, o_ref):
    for c in range(o_ref.shape[0]):
        blk = res[:, c * LANES:(c + 1) * LANES]
        r = lax.rsqrt(jnp.mean(blk * blk, axis=-1, keepdims=True) + NORM_EPS)
        f = hf_ref[c]
        o_ref[c] = (blk * (f * r + (1.0 - f)) * hg_ref[c]).astype(BF16)


def _meta_inproj_kernel(x_ref, g_ref, w_ref, hg_ref, hf_ref, o_ref, wb_ref):
    wb = w_ref[...].astype(BF16)
    wb_ref[...] = wb
    xn = _rms_rows(x_ref[...], g_ref[...]).astype(BF16)
    _head_norm_store(jnp.dot(xn, wb, preferred_element_type=F32), hg_ref, hf_ref, o_ref)


def _meta_inproj(x_meta, gain, w, head_gain, head_flag):
    m, d = x_meta.shape
    n = w.shape[1]
    bn, vmem = TILES["meta_inproj"]["bn"], TILES["meta_inproj"]["vmem"]
    cpb = bn // LANES
    return pl.pallas_call(
        _meta_inproj_kernel,
        grid=(n // bn,),
        in_specs=[
            pl.BlockSpec((m, d), lambda j: (0, 0)),
            pl.BlockSpec((1, d), lambda j: (0, 0)),
            pl.BlockSpec((d, bn), lambda j: (0, j)),
            pl.BlockSpec((cpb, 1, LANES), lambda j: (j, 0, 0)),
            pl.BlockSpec((cpb, 1, LANES), lambda j: (j, 0, 0)),
        ],
        out_specs=[pl.BlockSpec((cpb, m, LANES), lambda j: (j, 0, 0)),
                   pl.BlockSpec((d, bn), lambda j: (0, j))],
        out_shape=[jax.ShapeDtypeStruct((n // LANES, m, LANES), BF16),
                   jax.ShapeDtypeStruct((d, n), BF16)],
        compiler_params=_cparams(("parallel",), vmem),
        name="meta_inproj",
    )(x_meta, gain, w, head_gain, head_flag)


def _inproj_kernel(x_ref, g_ref, w_ref, hg_ref, hf_ref, *refs, n_cast, nj, ntiles):
    cast_in, o_ref, cast_out = refs[:n_cast], refs[n_cast], refs[n_cast + 1:2 * n_cast + 1]
    xn_ref, raw_even, raw_odd = refs[-3:]
    s = pl.program_id(0)
    for src_ref, dst_ref in zip(cast_in, cast_out):
        dst_ref[...] = src_ref[...].astype(BF16)

    @pl.when(s == 0)
    def _():
        raw_odd[...] = jnp.zeros(raw_odd.shape, F32)

    @pl.when((s % nj == 0) & (s < ntiles))
    def _():
        xn_ref[...] = _rms_rows(x_ref[...], g_ref[...]).astype(BF16)

    def step(cur_ref, prev_ref):
        def body():
            cur_ref[...] = jnp.dot(xn_ref[...], w_ref[...], preferred_element_type=F32)
            _head_norm_store(prev_ref, hg_ref, hf_ref, o_ref)
        return body

    pl.when(s % 2 == 0)(step(raw_even, raw_odd))
    pl.when(s % 2 == 1)(step(raw_odd, raw_even))


def _inproj(x2, gain, w, head_gain, head_flag, bm, casts=()):
    m, d = x2.shape
    n = w.shape[1]
    bn, vmem = TILES["inproj"]["bn"], TILES["inproj"]["vmem"]
    cpb = bn // LANES
    ni, nj = m // bm, n // bn
    ntiles = ni * nj

    def cur(s):
        return jnp.minimum(s, ntiles - 1)

    def prev(s):
        return jnp.maximum(s - 1, 0)

    cast_specs, cast_shapes = [], []
    for wc in casts:
        rows, cols = wc.shape
        nchunks = 1 << (ntiles.bit_length() - 1)
        rpc = rows // nchunks
        assert rows % nchunks == 0 and rpc % BF16_SUBLANES == 0
        cast_specs.append(pl.BlockSpec(
            (rpc, cols), lambda s, nchunks=nchunks: (jnp.minimum(s, nchunks - 1), 0)))
        cast_shapes.append(jax.ShapeDtypeStruct((rows, cols), BF16))
    outs = pl.pallas_call(
        functools.partial(_inproj_kernel, n_cast=len(casts), nj=nj, ntiles=ntiles),
        grid=(ntiles + 1,),
        in_specs=[
            pl.BlockSpec((bm, d), lambda s: (cur(s) // nj, 0)),
            pl.BlockSpec((1, d), lambda s: (0, 0)),
            pl.BlockSpec((d, bn), lambda s: (0, cur(s) % nj)),
            pl.BlockSpec((cpb, 1, LANES), lambda s: (prev(s) % nj, 0, 0)),
            pl.BlockSpec((cpb, 1, LANES), lambda s: (prev(s) % nj, 0, 0)),
        ] + cast_specs,
        out_specs=[pl.BlockSpec((cpb, bm, LANES), lambda s: (prev(s) % nj, prev(s) // nj, 0))]
                  + cast_specs,
        out_shape=[jax.ShapeDtypeStruct((n // LANES, m, LANES), BF16)] + cast_shapes,
        scratch_shapes=[pltpu.VMEM((bm, d), BF16), pltpu.VMEM((bm, bn), F32),
                        pltpu.VMEM((bm, bn), F32)],
        compiler_params=_cparams(("arbitrary",), vmem),
        name="inproj",
    )(x2, gain, w, head_gain, head_flag, *casts)
    return outs


def _mlp_up_kernel(a_ref, ss_ref, w_ref, o_ref, *, d):
    ss = ss_ref[0][:, :1]
    for c in range(1, ss_ref.shape[0]):
        ss = ss + ss_ref[c][:, :1]
    r = lax.rsqrt(ss * (1.0 / d) + NORM_EPS)
    u = jnp.dot(a_ref[...], w_ref[...], preferred_element_type=F32) * r
    u = jnp.maximum(u, 0.0)
    o_ref[...] = (u * u).astype(BF16)


def _mlp_up(a, ss, w):
    m, d = a.shape
    n = w.shape[1]
    ncol = ss.shape[0]
    bm, bn, vmem = (TILES["mlp_up"][k] for k in ("bm", "bn", "vmem"))
    return pl.pallas_call(
        functools.partial(_mlp_up_kernel, d=d),
        grid=(m // bm, n // bn),
        in_specs=[
            pl.BlockSpec((bm, d), lambda i, j: (i, 0)),
            pl.BlockSpec((ncol, bm, LANES), lambda i, j: (0, i, 0)),
            pl.BlockSpec((d, bn), lambda i, j: (0, j)),
        ],
        out_specs=pl.BlockSpec((bm, bn), lambda i, j: (i, j)),
        out_shape=jax.ShapeDtypeStruct((m, n), BF16),
        compiler_params=_cparams(("parallel", "parallel"), vmem),
        name="mlp_up",
    )(a, ss, w)


def _mlp_down_kernel(a_ref, w_ref, r_ref, o_ref):
    @pl.when(pl.program_id(2) == 0)
    def _():
        o_ref[...] = r_ref[...]

    o_ref[...] += jnp.dot(a_ref[...], w_ref[...], preferred_element_type=F32)


def _mlp_down(a, w, resid):
    m, kd = a.shape
    n = w.shape[1]
    bm, bn, bk, vmem = (TILES["mlp_down"][k] for k in ("bm", "bn", "bk", "vmem"))
    return pl.pallas_call(
        _mlp_down_kernel,
        grid=(m // bm, n // bn, kd // bk),
        in_specs=[
            pl.BlockSpec((bm, bk), lambda i, j, k: (i, k)),
            pl.BlockSpec((bk, bn), lambda i, j, k: (k, j)),
            pl.BlockSpec((bm, bn), lambda i, j, k: (i, j)),
        ],
        out_specs=pl.BlockSpec((bm, bn), lambda i, j, k: (i, j)),
        out_shape=jax.ShapeDtypeStruct((m, n), F32),
        compiler_params=_cparams(("parallel", "parallel", "arbitrary"), vmem),
        name="mlp_down",
    )(a, w, resid)


def _outproj_kernel(na_ref, sw_ref, wt_ref, wb_ref, r_ref, g_ref, o_ref, ob_ref, ss_ref):
    acc = jnp.dot(na_ref[...], wt_ref[...], preferred_element_type=F32)
    acc = acc + jnp.dot(sw_ref[...], wb_ref[...], preferred_element_type=F32)
    x1 = r_ref[...] + acc
    o_ref[...] = x1
    ob_ref[...] = (x1 * g_ref[...]).astype(BF16)
    ss_ref[...] = jnp.broadcast_to(jnp.sum(x1 * x1, axis=-1, keepdims=True), ss_ref.shape)


def _outproj(na_o, sw_o, w, resid, gain):
    m, kh = na_o.shape
    n = w.shape[1]
    bm, bn, vmem = (TILES["outproj"][k] for k in ("bm", "bn", "vmem"))
    return pl.pallas_call(
        _outproj_kernel,
        grid=(n // bn, m // bm),
        in_specs=[
            pl.BlockSpec((bm, kh), lambda j, i: (i, 0)),
            pl.BlockSpec((bm, kh), lambda j, i: (i, 0)),
            pl.BlockSpec((kh, bn), lambda j, i: (0, j), pipeline_mode=pl.Buffered(1)),
            pl.BlockSpec((kh, bn), lambda j, i: (1, j), pipeline_mode=pl.Buffered(1)),
            pl.BlockSpec((bm, bn), lambda j, i: (i, j)),
            pl.BlockSpec((1, bn), lambda j, i: (0, j)),
        ],
        out_specs=[pl.BlockSpec((bm, bn), lambda j, i: (i, j)),
                   pl.BlockSpec((bm, bn), lambda j, i: (i, j)),
                   pl.BlockSpec((None, bm, LANES), lambda j, i: (j, i, 0))],
        out_shape=[jax.ShapeDtypeStruct((m, n), F32),
                   jax.ShapeDtypeStruct((m, n), BF16),
                   jax.ShapeDtypeStruct((n // bn, m, LANES), F32)],
        compiler_params=_cparams(("parallel", "parallel"), vmem),
        name="outproj",
    )(na_o, sw_o, w, w, resid, gain)


def _qkt(q, k):
    return lax.dot_general(q, k, (((1,), (1,)), ((), ())), preferred_element_type=F32)


def _softmax_pv(s, v):
    p = jnp.exp2(s - jnp.max(s, axis=-1, keepdims=True)).astype(BF16)
    o = jnp.dot(p, jnp.concatenate([v, jnp.ones_like(v)], axis=1), preferred_element_type=F32)
    return (o[:, :LANES] / o[:, LANES:]).astype(BF16)


def _na_plan():
    vblk = 6
    vrows = vblk * NA_QROWS
    wrows = NA_KBLKS * NA_QROWS
    specs, plans = [], []
    for j in (0, 2, vblk - 1):
        ws = int(np.clip(j - 1, 0, vblk - NA_KBLKS))
        meta_row = 0 if j == vblk - 1 else wrows - 1
        plan = []
        for qrl in range(NA_QROWS):
            qr = NA_QROWS * j + qrl
            rs = int(np.clip(qr - NA_ROWS // 2, 0, vrows - NA_ROWS))
            row = []
            for t in range(NA_TILES_PER_ROW):
                halves = []
                for krl in (2 * t, 2 * t + 1):
                    kr = NA_QROWS * ws + krl
                    half = kr - qr if rs <= kr < rs + NA_ROWS else None
                    if krl == meta_row:
                        assert half is None
                        half = "meta"
                    halves.append(half)
                halves = tuple(halves)
                if halves == (None, None):
                    row.append(-1)
                else:
                    if halves not in specs:
                        specs.append(halves)
                    row.append(specs.index(halves))
            plan.append(row)
        plans.append(plan)
    return specs, plans


def _na_tiles(rpb, specs):
    nh = rpb.shape[0]
    c = np.arange(GRID_W)
    cs = np.clip(c - NA_COLS // 2, 0, GRID_W - NA_COLS)
    col_in = (c[None, :] >= cs[:, None]) & (c[None, :] < cs[:, None] + NA_COLS)
    dc = np.clip(c[None, :] - c[:, None], -(NA_COLS - 1), NA_COLS - 1) + (NA_COLS - 1)
    onehot = np.zeros((2 * NA_COLS - 1, GRID_W * GRID_W), np.float32)
    onehot[dc.reshape(-1), np.arange(GRID_W * GRID_W)] = 1.0
    tcol = jnp.einsum("hab,bx->hax", rpb.astype(F32), onehot, precision=lax.Precision.HIGHEST)
    tcol = jnp.where(col_in[None, None], tcol.reshape(nh, 2 * NA_ROWS - 1, GRID_W, GRID_W), NEG_INF)
    masked = jnp.full((nh, GRID_W, GRID_W), NEG_INF, F32)
    meta = jnp.where(np.arange(GRID_W) < N_META, 0.0, masked)

    def half(dr):
        if dr is None:
            return masked
        return meta if dr == "meta" else tcol[:, dr + NA_ROWS - 1]

    tiles = jnp.stack([jnp.concatenate([half(l), half(r)], axis=-1) for l, r in specs], axis=1)
    return tiles * LOG2_E


def _na_kernel(q_ref, k0_ref, k1_ref, k2_ref, v0_ref, v1_ref, v2_ref, km_ref, vm_ref, tiles_ref,
               o_ref, bias_ref, k_scr, v_scr, *, scale, plans):
    hb = q_ref.shape[0]
    j = pl.program_id(1)
    last = pl.num_programs(1) - 1
    first_row = pl.program_id(2) == 0
    masked_tile = jnp.full((GRID_W, LANES), NEG_INF, F32)

    def build(plan):
        def body():
            for hh in range(hb):
                for qrl in range(NA_QROWS):
                    for t in range(NA_TILES_PER_ROW):
                        idx = plan[qrl][t]
                        tile = masked_tile if idx < 0 else tiles_ref[hh, idx]
                        bias_ref[hh, qrl * GRID_W:(qrl + 1) * GRID_W, t * LANES:(t + 1) * LANES] = tile
        return body

    pl.when(first_row & (j == 0))(build(plans[0]))
    pl.when(first_row & (j == 1))(build(plans[1]))
    pl.when(first_row & (j == last))(build(plans[2]))

    meta_pos = pl.multiple_of(
        jnp.where(j == last, 0, (NA_KBLKS * NA_QROWS - 1) * GRID_W), N_META)
    for hh in range(hb):
        for d, (kb_ref, vb_ref) in enumerate(((k0_ref, v0_ref), (k1_ref, v1_ref), (k2_ref, v2_ref))):
            k_scr[hh, d * NA_QBLK:(d + 1) * NA_QBLK, :] = kb_ref[hh]
            v_scr[hh, d * NA_QBLK:(d + 1) * NA_QBLK, :] = vb_ref[hh]
        k_scr[hh, pl.ds(meta_pos, N_META), :] = km_ref[hh]
        v_scr[hh, pl.ds(meta_pos, N_META), :] = vm_ref[hh]
        q = q_ref[hh]
        k = k_scr[hh]
        v = v_scr[hh]
        s = _qkt(q, k) * (scale * LOG2_E) + bias_ref[hh]
        o_ref[:, hh * LANES:(hh + 1) * LANES] = _softmax_pv(s, v)


def _na_attention(proj, km, vm, tiles, plans, batch, rows):
    nh = km.shape[0]
    hb = NA_HEADS_PER_STEP
    nblk = rows // NA_QROWS
    tokens = proj.shape[1]
    assert nblk >= NA_KBLKS and nh % hb == 0
    hblks = nh // hb

    def kv_map(base, d):
        def f(h, j, b):
            ws = jnp.clip(j - 1, 0, nblk - NA_KBLKS)
            return (base + h, b * nblk + ws + d, 0)
        return f

    qspec = pl.BlockSpec((hb, NA_QBLK, LANES), lambda h, j, b: (h, b * nblk + j, 0))
    kspecs = [pl.BlockSpec((hb, NA_QBLK, LANES), kv_map(hblks, d)) for d in range(NA_KBLKS)]
    vspecs = [pl.BlockSpec((hb, NA_QBLK, LANES), kv_map(2 * hblks, d)) for d in range(NA_KBLKS)]
    mspec = pl.BlockSpec((hb, N_META, LANES), lambda h, j, b: (h, 0, 0))
    tspec = pl.BlockSpec((hb,) + tiles.shape[1:], lambda h, j, b: (h, 0, 0, 0))
    return pl.pallas_call(
        functools.partial(_na_kernel, scale=HEAD_DIM ** -0.5, plans=plans),
        grid=(hblks, nblk, batch),
        in_specs=[qspec] + kspecs + vspecs + [mspec, mspec, tspec],
        out_specs=pl.BlockSpec((NA_QBLK, hb * LANES), lambda h, j, b: (b * nblk + j, h)),
        out_shape=jax.ShapeDtypeStruct((tokens, nh * LANES), BF16),
        scratch_shapes=[pltpu.VMEM((hb, NA_QBLK, NA_KBLKS * NA_QBLK), F32),
                        pltpu.VMEM((hb, NA_KBLKS * NA_QBLK, LANES), BF16),
                        pltpu.VMEM((hb, NA_KBLKS * NA_QBLK, LANES), BF16)],
        compiler_params=_cparams(("arbitrary", "arbitrary", "arbitrary"), ATTN_VMEM_MIB["na"]),
        name="na_attn",
    )(proj, proj, proj, proj, proj, proj, proj, km, vm, tiles)


def _swa_kernel(q_ref, k0_ref, k1_ref, k2_ref, v0_ref, v1_ref, v2_ref, km_ref, vm_ref, bias_ref,
                o_ref, *, scale, group):
    hkb = k0_ref.shape[0]
    blk = q_ref.shape[1]
    for kh in range(hkb):
        heads = slice(kh * group, (kh + 1) * group)
        q = q_ref[heads].reshape(group * blk, LANES)
        k = jnp.concatenate([k0_ref[kh], k1_ref[kh], k2_ref[kh], km_ref[kh]], axis=0)
        v = jnp.concatenate([v0_ref[kh], v1_ref[kh], v2_ref[kh], vm_ref[kh]], axis=0)
        s = _qkt(q, k) * (scale * LOG2_E) + bias_ref[heads].reshape(group * blk, 3 * blk + LANES)
        o = _softmax_pv(s, v)
        for g in range(group):
            col = (kh * group + g) * LANES
            o_ref[:, col:col + LANES] = o[g * blk:(g + 1) * blk]


def _swa_attention(proj, km, vm, bias, batch, seq, q_base, k_base, v_base):
    hkv = km.shape[0]
    hq = bias.shape[1]
    group = hq // hkv
    hkb = SWA_KV_HEADS_PER_STEP
    blk = SWA_BLOCK
    nb = seq // blk
    tokens = proj.shape[1]
    assert hkv % hkb == 0 and k_base % hkb == 0 and v_base % hkb == 0 and q_base % (hkb * group) == 0

    def bias_map(hk, b, i):
        variant = jnp.where(i == 0, 1, 0) + jnp.where(i == nb - 1, 2, 0)
        return (variant, hk, 0, 0)

    def kv_map(base, d):
        def f(hk, b, i):
            return (base // hkb + hk, b * nb + jnp.clip(i - 1 + d, 0, nb - 1), 0)
        return f

    kspecs = [pl.BlockSpec((hkb, blk, LANES), kv_map(k_base, d)) for d in range(3)]
    vspecs = [pl.BlockSpec((hkb, blk, LANES), kv_map(v_base, d)) for d in range(3)]
    mspec = pl.BlockSpec((hkb, LANES, LANES), lambda hk, b, i: (hk, 0, 0))
    return pl.pallas_call(
        functools.partial(_swa_kernel, scale=HEAD_DIM ** -0.5, group=group),
        grid=(hkv // hkb, batch, nb),
        in_specs=[pl.BlockSpec((hkb * group, blk, LANES),
                               lambda hk, b, i: (q_base // (hkb * group) + hk, b * nb + i, 0))]
                 + kspecs + vspecs + [mspec, mspec,
                  pl.BlockSpec((None, hkb * group, blk, 3 * blk + LANES), bias_map)],
        out_specs=pl.BlockSpec((blk, hkb * group * LANES), lambda hk, b, i: (b * nb + i, hk)),
        out_shape=jax.ShapeDtypeStruct((tokens, hq * LANES), BF16),
        compiler_params=_cparams(("parallel", "parallel", "parallel"), ATTN_VMEM_MIB["swa"]),
        name="swa_attn",
    )(proj, proj, proj, proj, proj, proj, proj, km, vm, bias)


def _t5_bucket(rel):
    nb = T5_BUCKETS // 2
    max_exact = nb // 2
    ret = np.where(rel > 0, nb, 0)
    n = np.abs(rel)
    large = max_exact + (np.log(np.maximum(n, 1) / max_exact)
                         / math.log(T5_MAX_DIST / max_exact) * (nb - max_exact)).astype(np.int64)
    large = np.minimum(large, nb - 1)
    return ret + np.where(n < max_exact, n, large)


def _select_rows(table, idx):
    flat = idx.reshape(-1)
    onehot = np.zeros((table.shape[0], flat.size), np.float32)
    onehot[flat, np.arange(flat.size)] = 1.0
    out = jnp.einsum("bh,bx->xh", table.astype(F32), onehot, precision=lax.Precision.HIGHEST)
    return out.reshape(idx.shape + (table.shape[1],))


def _swa_tables(t5_bias, sink):
    blk = SWA_BLOCK
    rel = np.arange(-(2 * blk - 1), 2 * blk)
    by_rel = _select_rows(t5_bias, _t5_bucket(rel)).T
    by_rel = jnp.where((np.abs(rel) <= SWA_WINDOW)[None], by_rel, NEG_INF)
    bias_w = jnp.stack([by_rel[:, blk - 1 - q:blk - 1 - q + 3 * blk] for q in range(blk)], axis=1)
    jj = np.arange(3 * blk)
    edge = np.stack([np.zeros_like(jj, bool), jj < blk, jj >= 2 * blk, (jj < blk) | (jj >= 2 * blk)])
    bias_w = jnp.where(edge[:, None, None, :], NEG_INF, bias_w[None])
    assert N_META + blk - (N_META - 1) > T5_MAX_DIST
    rel_m = np.arange(N_META)[None, :] - (N_META + np.arange(2 * blk))[:, None]
    bias_m = _select_rows(t5_bias, _t5_bucket(rel_m))
    bias_m = bias_m.reshape(2, blk, N_META, -1).transpose(0, 3, 1, 2)
    bias_m = jnp.stack([bias_m[1], bias_m[0], bias_m[1], bias_m[0]])
    hq = bias_m.shape[1]
    sink_col = jnp.broadcast_to(sink.astype(F32)[None, :, None, None], (4, hq, blk, 1))
    pad = jnp.full((4, hq, blk, LANES - N_META - 1), NEG_INF, F32)
    return jnp.concatenate([bias_w, bias_m, sink_col, pad], axis=-1) * LOG2_E


def _trunk(x, meta_kv, tabs, params, later_weights):
    batch, seq, d = x.shape
    (g_attn, w_in, head_gain, head_flag, g_mlp) = params
    km_na, vm_na, km_sw, vm_sw = meta_kv
    na_tiles, na_plans, swa_bias = tabs
    nh_na = km_na.shape[0]
    hq_sw = swa_bias.shape[1]
    hkv_sw = km_sw.shape[0]
    x2 = x.reshape(batch * seq, d)
    casts = () if later_weights[0].dtype == BF16 else later_weights
    proj, *cast_out = _inproj(x2, g_attn, w_in, head_gain, head_flag, TILES["inproj"]["bm"], casts)
    w_out, w_up, w_down = later_weights = tuple(cast_out) if cast_out else later_weights
    rows = seq // GRID_W
    assert seq % (GRID_W * NA_QROWS) == 0 and rows >= NA_ROWS
    assert seq % SWA_BLOCK == 0
    na_o = _na_attention(proj, km_na, vm_na, na_tiles, na_plans, batch, rows)
    q_base = 3 * nh_na
    k_base = q_base + hq_sw
    v_base = k_base + hkv_sw
    sw_o = _swa_attention(proj, km_sw, vm_sw, swa_bias, batch, seq, q_base, k_base, v_base)
    x1, x1g, x1ss = _outproj(na_o, sw_o, w_out, x2, g_mlp)
    u = _mlp_up(x1g, x1ss, w_up)
    y = _mlp_down(u, w_down, x1)
    return y.reshape(batch, seq, d), later_weights


def kernel(x_prompt, x_sample, meta_tokens, t5_bias, norm_attn, w_in, q_norm_na, k_norm_na, na_rpb,
           q_norm_swa, k_norm_swa, swa_sink, w_out, norm_mlp, w_up, w_down):
    depth = w_in.shape[0]
    assert depth == 1
    nh_na = na_rpb.shape[1]
    hq_sw = swa_sink.shape[1]
    in_width = w_in.shape[2]
    hkv_sw = (in_width // HEAD_DIM - 3 * nh_na - hq_sw) // 2

    ones = jnp.ones((HEAD_DIM,), F32)

    def rep(g, n):
        return jnp.broadcast_to(g.astype(F32)[None], (n, HEAD_DIM))

    head_gain = jnp.concatenate([
        rep(q_norm_na[0], nh_na), rep(k_norm_na[0], nh_na), rep(ones, nh_na),
        rep(q_norm_swa[0], hq_sw), rep(k_norm_swa[0], hkv_sw), rep(ones, hkv_sw)])[:, None, :]
    flag = np.concatenate([np.ones(2 * nh_na), np.zeros(nh_na), np.ones(hq_sw + hkv_sw),
                           np.zeros(hkv_sw)]).astype(np.float32)
    head_flag = jnp.asarray(np.broadcast_to(flag[:, None, None], (flag.shape[0], 1, HEAD_DIM)))

    g_attn = norm_attn[0].astype(F32)[None]
    g_mlp = norm_mlp[0].astype(F32)[None]

    mproj, w_in_b = _meta_inproj(meta_tokens.astype(F32), g_attn, w_in[0].astype(F32),
                                 head_gain, head_flag)
    k0 = 3 * nh_na + hq_sw
    mpad = jnp.pad(mproj[k0:k0 + 2 * hkv_sw], ((0, 0), (0, LANES - N_META), (0, 0)))
    meta_kv = (mproj[nh_na:2 * nh_na], mproj[2 * nh_na:3 * nh_na], mpad[:hkv_sw], mpad[hkv_sw:])

    specs, plans = _na_plan()
    tabs = (_na_tiles(na_rpb[0], specs), plans, _swa_tables(t5_bias, swa_sink[0]))
    params = (g_attn, w_in_b, head_gain, head_flag, g_mlp)
    weights_f32 = (w_out[0].astype(F32), w_up[0].astype(F32), w_down[0].astype(F32))
    y_sample, weights_bf16 = _trunk(x_sample, meta_kv, tabs, params, weights_f32)
    y_prompt, _ = _trunk(x_prompt, meta_kv, tabs, params, weights_bf16)
    return (y_prompt, y_sample)
```

```python
import functools
import math

import jax
import jax.numpy as jnp
import numpy as np
from jax import lax
from jax.experimental import pallas as pl
from jax.experimental.pallas import tpu as pltpu

F32 = jnp.float32
BF16 = jnp.bfloat16

HEAD_DIM = 128
N_META = 16
GRID_W = 64
NA_ROWS = 8
NA_COLS = 16
SWA_WINDOW = 128
SWA_BLOCK = 128
T5_BUCKETS = 32
T5_MAX_DIST = 128
NORM_EPS = 1e-6
NEG_INF = -1e30
LOG2_E = math.log2(math.e)

LANES = 128
NA_QROWS = 4
NA_QBLK = NA_QROWS * GRID_W
NA_KBLKS = 3
NA_HEADS_PER_STEP = 16
SWA_KV_HEADS_PER_STEP = 4
NA_TILES_PER_ROW = NA_KBLKS * NA_QBLK // LANES
MIB = 1024 * 1024
V7X_VMEM_MIB = 64
BF16_SUBLANES = 16

TILES = {
    "meta_inproj": dict(bn=512, vmem=40),
    "inproj": dict(bm=512, bn=1024, vmem=60),
    "outproj": dict(bm=512, bn=2048, vmem=56),
    "mlp_up": dict(bm=1024, bn=1024, vmem=52),
    "mlp_down": dict(bm=1024, bn=1024, bk=4096, vmem=60),
}
ATTN_VMEM_MIB = dict(na=56, swa=48)

assert 2 * GRID_W == LANES
assert all(t["vmem"] < V7X_VMEM_MIB for t in TILES.values())


def _cparams(sem, vmem_mib):
    return pltpu.CompilerParams(dimension_semantics=sem, vmem_limit_bytes=vmem_mib * MIB)


def _rms_rows(x, gain):
    ms = jnp.mean(x * x, axis=-1, keepdims=True)
    return x * lax.rsqrt(ms + NORM_EPS) * gain


def _head_norm_store(res, hg_ref, hf_ref, o_ref):
    for c in range(o_ref.shape[0]):
        blk = res[:, c * LANES:(c + 1) * LANES]
        r = lax.rsqrt(jnp.mean(blk * blk, axis=-1, keepdims=True) + NORM_EPS)
        f = hf_ref[c]
        o_ref[c] = (blk * (f * r + (1.0 - f)) * hg_ref[c]).astype(BF16)


def _meta_inproj_kernel(x_ref, g_ref, w_ref, hg_ref, hf_ref, o_ref, wb_ref):
    wb = w_ref[...].astype(BF16)
    wb_ref[...] = wb
    xn = _rms_rows(x_ref[...], g_ref[...]).astype(BF16)
    _head_norm_store(jnp.dot(xn, wb, preferred_element_type=F32), hg_ref, hf_ref, o_ref)


def _meta_inproj(x_meta, gain, w, head_gain, head_flag):
    m, d = x_meta.shape
    n = w.shape[1]
    bn, vmem = TILES["meta_inproj"]["bn"], TILES["meta_inproj"]["vmem"]
    cpb = bn // LANES
    return pl.pallas_call(
        _meta_inproj_kernel,
        grid=(n // bn,),
        in_specs=[
            pl.BlockSpec((m, d), lambda j: (0, 0)),
            pl.BlockSpec((1, d), lambda j: (0, 0)),
            pl.BlockSpec((d, bn), lambda j: (0, j)),
            pl.BlockSpec((cpb, 1, LANES), lambda j: (j, 0, 0)),
            pl.BlockSpec((cpb, 1, LANES), lambda j: (j, 0, 0)),
        ],
        out_specs=[pl.BlockSpec((cpb, m, LANES), lambda j: (j, 0, 0)),
                   pl.BlockSpec((d, bn), lambda j: (0, j))],
        out_shape=[jax.ShapeDtypeStruct((n // LANES, m, LANES), BF16),
                   jax.ShapeDtypeStruct((d, n), BF16)],
        compiler_params=_cparams(("parallel",), vmem),
        name="meta_inproj",
    )(x_meta, gain, w, head_gain, head_flag)


def _inproj_kernel(x_ref, g_ref, w_ref, hg_ref, hf_ref, *refs, n_cast, nj, ntiles):
    cast_in, o_ref, cast_out = refs[:n_cast], refs[n_cast], refs[n_cast + 1:2 * n_cast + 1]
    xn_ref, raw_even, raw_odd = refs[-3:]
    s = pl.program_id(0)
    for src_ref, dst_ref in zip(cast_in, cast_out):
        dst_ref[...] = src_ref[...].astype(BF16)

    @pl.when(s == 0)
    def _():
        raw_odd[...] = jnp.zeros(raw_odd.shape, F32)

    @pl.when((s % nj == 0) & (s < ntiles))
    def _():
        xn_ref[...] = _rms_rows(x_ref[...], g_ref[...]).astype(BF16)

    def step(cur_ref, prev_ref):
        def body():
            cur_ref[...] = jnp.dot(xn_ref[...], w_ref[...], preferred_element_type=F32)
            _head_norm_store(prev_ref, hg_ref, hf_ref, o_ref)
        return body

    pl.when(s % 2 == 0)(step(raw_even, raw_odd))
    pl.when(s % 2 == 1)(step(raw_odd, raw_even))


def _inproj(x2, gain, w, head_gain, head_flag, bm, casts=()):
    m, d = x2.shape
    n = w.shape[1]
    bn, vmem = TILES["inproj"]["bn"], TILES["inproj"]["vmem"]
    cpb = bn // LANES
    ni, nj = m // bm, n // bn
    ntiles = ni * nj

    def cur(s):
        return jnp.minimum(s, ntiles - 1)

    def prev(s):
        return jnp.maximum(s - 1, 0)

    cast_specs, cast_shapes = [], []
    for wc in casts:
        rows, cols = wc.shape
        nchunks = 1 << (ntiles.bit_length() - 1)
        rpc = rows // nchunks
        assert rows % nchunks == 0 and rpc % BF16_SUBLANES == 0
        cast_specs.append(pl.BlockSpec(
            (rpc, cols), lambda s, nchunks=nchunks: (jnp.minimum(s, nchunks - 1), 0)))
        cast_shapes.append(jax.ShapeDtypeStruct((rows, cols), BF16))
    outs = pl.pallas_call(
        functools.partial(_inproj_kernel, n_cast=len(casts), nj=nj, ntiles=ntiles),
        grid=(ntiles + 1,),
        in_specs=[
            pl.BlockSpec((bm, d), lambda s: (cur(s) // nj, 0)),
            pl.BlockSpec((1, d), lambda s: (0, 0)),
            pl.BlockSpec((d, bn), lambda s: (0, cur(s) % nj)),
            pl.BlockSpec((cpb, 1, LANES), lambda s: (prev(s) % nj, 0, 0)),
            pl.BlockSpec((cpb, 1, LANES), lambda s: (prev(s) % nj, 0, 0)),
        ] + cast_specs,
        out_specs=[pl.BlockSpec((cpb, bm, LANES), lambda s: (prev(s) % nj, prev(s) // nj, 0))]
                  + cast_specs,
        out_shape=[jax.ShapeDtypeStruct((n // LANES, m, LANES), BF16)] + cast_shapes,
        scratch_shapes=[pltpu.VMEM((bm, d), BF16), pltpu.VMEM((bm, bn), F32),
                        pltpu.VMEM((bm, bn), F32)],
        compiler_params=_cparams(("arbitrary",), vmem),
        name="inproj",
    )(x2, gain, w, head_gain, head_flag, *casts)
    return outs


def _mlp_up_kernel(a_ref, ss_ref, w_ref, o_ref, *, d):
    ss = ss_ref[0][:, :1]
    for c in range(1, ss_ref.shape[0]):
        ss = ss + ss_ref[c][:, :1]
    r = lax.rsqrt(ss * (1.0 / d) + NORM_EPS)
    u = jnp.dot(a_ref[...], w_ref[...], preferred_element_type=F32) * r
    u = jnp.maximum(u, 0.0)
    o_ref[...] = (u * u).astype(BF16)


def _mlp_up(a, ss, w):
    m, d = a.shape
    n = w.shape[1]
    ncol = ss.shape[0]
    bm, bn, vmem = (TILES["mlp_up"][k] for k in ("bm", "bn", "vmem"))
    return pl.pallas_call(
        functools.partial(_mlp_up_kernel, d=d),
        grid=(m // bm, n // bn),
        in_specs=[
            pl.BlockSpec((bm, d), lambda i, j: (i, 0)),
            pl.BlockSpec((ncol, bm, LANES), lambda i, j: (0, i, 0)),
            pl.BlockSpec((d, bn), lambda i, j: (0, j)),
        ],
        out_specs=pl.BlockSpec((bm, bn), lambda i, j: (i, j)),
        out_shape=jax.ShapeDtypeStruct((m, n), BF16),
        compiler_params=_cparams(("parallel", "parallel"), vmem),
        name="mlp_up",
    )(a, ss, w)


def _mlp_down_kernel(a_ref, w_ref, r_ref, o_ref):
    @pl.when(pl.program_id(2) == 0)
    def _():
        o_ref[...] = r_ref[...]

    o_ref[...] += jnp.dot(a_ref[...], w_ref[...], preferred_element_type=F32)


def _mlp_down(a, w, resid):
    m, kd = a.shape
    n = w.shape[1]
    bm, bn, bk, vmem = (TILES["mlp_down"][k] for k in ("bm", "bn", "bk", "vmem"))
    return pl.pallas_call(
        _mlp_down_kernel,
        grid=(m // bm, n // bn, kd // bk),
        in_specs=[
            pl.BlockSpec((bm, bk), lambda i, j, k: (i, k)),
            pl.BlockSpec((bk, bn), lambda i, j, k: (k, j)),
            pl.BlockSpec((bm, bn), lambda i, j, k: (i, j)),
        ],
        out_specs=pl.BlockSpec((bm, bn), lambda i, j, k: (i, j)),
        out_shape=jax.ShapeDtypeStruct((m, n), F32),
        compiler_params=_cparams(("parallel", "parallel", "arbitrary"), vmem),
        name="mlp_down",
    )(a, w, resid)


def _outproj_kernel(na_ref, sw_ref, wt_ref, wb_ref, r_ref, g_ref, o_ref, ob_ref, ss_ref):
    acc = jnp.dot(na_ref[...], wt_ref[...], preferred_element_type=F32)
    acc = acc + jnp.dot(sw_ref[...], wb_ref[...], preferred_element_type=F32)
    x1 = r_ref[...] + acc
    o_ref[...] = x1
    ob_ref[...] = (x1 * g_ref[...]).astype(BF16)
    ss_ref[...] = jnp.broadcast_to(jnp.sum(x1 * x1, axis=-1, keepdims=True), ss_ref.shape)


def _outproj(na_o, sw_o, w, resid, gain):
    m, kh = na_o.shape
    n = w.shape[1]
    bm, bn, vmem = (TILES["outproj"][k] for k in ("bm", "bn", "vmem"))
    return pl.pallas_call(
        _outproj_kernel,
        grid=(n // bn, m // bm),
        in_specs=[
            pl.BlockSpec((bm, kh), lambda j, i: (i, 0)),
            pl.BlockSpec((bm, kh), lambda j, i: (i, 0)),
            pl.BlockSpec((kh, bn), lambda j, i: (0, j), pipeline_mode=pl.Buffered(1)),
            pl.BlockSpec((kh, bn), lambda j, i: (1, j), pipeline_mode=pl.Buffered(1)),
            pl.BlockSpec((bm, bn), lambda j, i: (i, j)),
            pl.BlockSpec((1, bn), lambda j, i: (0, j)),
        ],
        out_specs=[pl.BlockSpec((bm, bn), lambda j, i: (i, j)),
                   pl.BlockSpec((bm, bn), lambda j, i: (i, j)),
                   pl.BlockSpec((None, bm, LANES), lambda j, i: (j, i, 0))],
        out_shape=[jax.ShapeDtypeStruct((m, n), F32),
                   jax.ShapeDtypeStruct((m, n), BF16),
                   jax.ShapeDtypeStruct((n // bn, m, LANES), F32)],
        compiler_params=_cparams(("parallel", "parallel"), vmem),
        name="outproj",
    )(na_o, sw_o, w, w, resid, gain)


def _qkt(q, k):
    return lax.dot_general(q, k, (((1,), (1,)), ((), ())), preferred_element_type=F32)


def _softmax_pv(s, v):
    p = jnp.exp2(s - jnp.max(s, axis=-1, keepdims=True)).astype(BF16)
    o = jnp.dot(p, jnp.concatenate([v, jnp.ones_like(v)], axis=1), preferred_element_type=F32)
    return (o[:, :LANES] / o[:, LANES:]).astype(BF16)


def _na_plan():
    vblk = 6
    vrows = vblk * NA_QROWS
    wrows = NA_KBLKS * NA_QROWS
    specs, plans = [], []
    for j in (0, 2, vblk - 1):
        ws = int(np.clip(j - 1, 0, vblk - NA_KBLKS))
        meta_row = 0 if j == vblk - 1 else wrows - 1
        plan = []
        for qrl in range(NA_QROWS):
            qr = NA_QROWS * j + qrl
            rs = int(np.clip(qr - NA_ROWS // 2, 0, vrows - NA_ROWS))
            row = []
            for t in range(NA_TILES_PER_ROW):
                halves = []
                for krl in (2 * t, 2 * t + 1):
                    kr = NA_QROWS * ws + krl
                    half = kr - qr if rs <= kr < rs + NA_ROWS else None
                    if krl == meta_row:
                        assert half is None
                        half = "meta"
                    halves.append(half)
                halves = tuple(halves)
                if halves == (None, None):
                    row.append(-1)
                else:
                    if halves not in specs:
                        specs.append(halves)
                    row.append(specs.index(halves))
            plan.append(row)
        plans.append(plan)
    return specs, plans


def _na_tiles(rpb, specs):
    nh = rpb.shape[0]
    c = np.arange(GRID_W)
    cs = np.clip(c - NA_COLS // 2, 0, GRID_W - NA_COLS)
    col_in = (c[None, :] >= cs[:, None]) & (c[None, :] < cs[:, None] + NA_COLS)
    dc = np.clip(c[None, :] - c[:, None], -(NA_COLS - 1), NA_COLS - 1) + (NA_COLS - 1)
    onehot = np.zeros((2 * NA_COLS - 1, GRID_W * GRID_W), np.float32)
    onehot[dc.reshape(-1), np.arange(GRID_W * GRID_W)] = 1.0
    tcol = jnp.einsum("hab,bx->hax", rpb.astype(F32), onehot, precision=lax.Precision.HIGHEST)
    tcol = jnp.where(col_in[None, None], tcol.reshape(nh, 2 * NA_ROWS - 1, GRID_W, GRID_W), NEG_INF)
    masked = jnp.full((nh, GRID_W, GRID_W), NEG_INF, F32)
    meta = jnp.where(np.arange(GRID_W) < N_META, 0.0, masked)

    def half(dr):
        if dr is None:
            return masked
        return meta if dr == "meta" else tcol[:, dr + NA_ROWS - 1]

    tiles = jnp.stack([jnp.concatenate([half(l), half(r)], axis=-1) for l, r in specs], axis=1)
    return tiles * LOG2_E


def _na_kernel(q_ref, k0_ref, k1_ref, k2_ref, v0_ref, v1_ref, v2_ref, km_ref, vm_ref, tiles_ref,
               o_ref, bias_ref, k_scr, v_scr, *, scale, plans):
    hb = q_ref.shape[0]
    j = pl.program_id(1)
    last = pl.num_programs(1) - 1
    first_row = pl.program_id(2) == 0
    masked_tile = jnp.full((GRID_W, LANES), NEG_INF, F32)

    def build(plan):
        def body():
            for hh in range(hb):
                for qrl in range(NA_QROWS):
                    for t in range(NA_TILES_PER_ROW):
                        idx = plan[qrl][t]
                        tile = masked_tile if idx < 0 else tiles_ref[hh, idx]
                        bias_ref[hh, qrl * GRID_W:(qrl + 1) * GRID_W, t * LANES:(t + 1) * LANES] = tile
        return body

    pl.when(first_row & (j == 0))(build(plans[0]))
    pl.when(first_row & (j == 1))(build(plans[1]))
    pl.when(first_row & (j == last))(build(plans[2]))

    meta_pos = pl.multiple_of(
        jnp.where(j == last, 0, (NA_KBLKS * NA_QROWS - 1) * GRID_W), N_META)
    for hh in range(hb):
        for d, (kb_ref, vb_ref) in enumerate(((k0_ref, v0_ref), (k1_ref, v1_ref), (k2_ref, v2_ref))):
            k_scr[hh, d * NA_QBLK:(d + 1) * NA_QBLK, :] = kb_ref[hh]
            v_scr[hh, d * NA_QBLK:(d + 1) * NA_QBLK, :] = vb_ref[hh]
        k_scr[hh, pl.ds(meta_pos, N_META), :] = km_ref[hh]
        v_scr[hh, pl.ds(meta_pos, N_META), :] = vm_ref[hh]
        q = q_ref[hh]
        k = k_scr[hh]
        v = v_scr[hh]
        s = _qkt(q, k) * (scale * LOG2_E) + bias_ref[hh]
        o_ref[:, hh * LANES:(hh + 1) * LANES] = _softmax_pv(s, v)


def _na_attention(proj, km, vm, tiles, plans, batch, rows):
    nh = km.shape[0]
    hb = NA_HEADS_PER_STEP
    nblk = rows // NA_QROWS
    tokens = proj.shape[1]
    assert nblk >= NA_KBLKS and nh % hb == 0
    hblks = nh // hb

    def kv_map(base, d):
        def f(h, j, b):
            ws = jnp.clip(j - 1, 0, nblk - NA_KBLKS)
            return (base + h, b * nblk + ws + d, 0)
        return f

    qspec = pl.BlockSpec((hb, NA_QBLK, LANES), lambda h, j, b: (h, b * nblk + j, 0))
    kspecs = [pl.BlockSpec((hb, NA_QBLK, LANES), kv_map(hblks, d)) for d in range(NA_KBLKS)]
    vspecs = [pl.BlockSpec((hb, NA_QBLK, LANES), kv_map(2 * hblks, d)) for d in range(NA_KBLKS)]
    mspec = pl.BlockSpec((hb, N_META, LANES), lambda h, j, b: (h, 0, 0))
    tspec = pl.BlockSpec((hb,) + tiles.shape[1:], lambda h, j, b: (h, 0, 0, 0))
    return pl.pallas_call(
        functools.partial(_na_kernel, scale=HEAD_DIM ** -0.5, plans=plans),
        grid=(hblks, nblk, batch),
        in_specs=[qspec] + kspecs + vspecs + [mspec, mspec, tspec],
        out_specs=pl.BlockSpec((NA_QBLK, hb * LANES), lambda h, j, b: (b * nblk + j, h)),
        out_shape=jax.ShapeDtypeStruct((tokens, nh * LANES), BF16),
        scratch_shapes=[pltpu.VMEM((hb, NA_QBLK, NA_KBLKS * NA_QBLK), F32),
                        pltpu.VMEM((hb, NA_KBLKS * NA_QBLK, LANES), BF16),
                        pltpu.VMEM((hb, NA_KBLKS * NA_QBLK, LANES), BF16)],
        compiler_params=_cparams(("arbitrary", "arbitrary", "arbitrary"), ATTN_VMEM_MIB["na"]),
        name="na_attn",
    )(proj, proj, proj, proj, proj, proj, proj, km, vm, tiles)


def _swa_kernel(q_ref, k0_ref, k1_ref, k2_ref, k3_ref, v0_ref, v1_ref, v2_ref, v3_ref, km_ref, vm_ref,
                bias0_ref, bias1_ref, o_ref, *, scale, group):
    hkb = k0_ref.shape[0]
    blk = k0_ref.shape[1]
    k_refs = (k0_ref, k1_ref, k2_ref, k3_ref)
    v_refs = (v0_ref, v1_ref, v2_ref, v3_ref)
    for kh in range(hkb):
        heads = slice(kh * group, (kh + 1) * group)
        for t, bias_ref in enumerate((bias0_ref, bias1_ref)):
            rows = slice(t * blk, (t + 1) * blk)
            q = q_ref[heads, rows, :].reshape(group * blk, LANES)
            k = jnp.concatenate([r[kh] for r in k_refs[t:t + 3]] + [km_ref[kh]], axis=0)
            v = jnp.concatenate([r[kh] for r in v_refs[t:t + 3]] + [vm_ref[kh]], axis=0)
            s = _qkt(q, k) * (scale * LOG2_E) + bias_ref[heads].reshape(group * blk, 3 * blk + LANES)
            o = _softmax_pv(s, v)
            for g in range(group):
                col = (kh * group + g) * LANES
                o_ref[rows, col:col + LANES] = o[g * blk:(g + 1) * blk]


def _swa_attention(proj, km, vm, bias, batch, seq, q_base, k_base, v_base):
    hkv = km.shape[0]
    hq = bias.shape[1]
    group = hq // hkv
    hkb = SWA_KV_HEADS_PER_STEP
    blk = SWA_BLOCK
    nb = seq // blk
    tokens = proj.shape[1]
    assert hkv % hkb == 0 and k_base % hkb == 0 and v_base % hkb == 0 and q_base % (hkb * group) == 0

    assert nb % 2 == 0
    npair = nb // 2

    def bias_map(t):
        def f(hk, b, i):
            blk_i = 2 * i + t
            variant = jnp.where(blk_i == 0, 1, 0) + jnp.where(blk_i == nb - 1, 2, 0)
            return (variant, hk, 0, 0)
        return f

    def kv_map(base, d):
        def f(hk, b, i):
            return (base // hkb + hk, b * nb + jnp.clip(2 * i - 1 + d, 0, nb - 1), 0)
        return f

    kspecs = [pl.BlockSpec((hkb, blk, LANES), kv_map(k_base, d)) for d in range(4)]
    vspecs = [pl.BlockSpec((hkb, blk, LANES), kv_map(v_base, d)) for d in range(4)]
    mspec = pl.BlockSpec((hkb, LANES, LANES), lambda hk, b, i: (hk, 0, 0))
    bspecs = [pl.BlockSpec((None, hkb * group, blk, 3 * blk + LANES), bias_map(t)) for t in range(2)]
    return pl.pallas_call(
        functools.partial(_swa_kernel, scale=HEAD_DIM ** -0.5, group=group),
        grid=(hkv // hkb, batch, npair),
        in_specs=[pl.BlockSpec((hkb * group, 2 * blk, LANES),
                               lambda hk, b, i: (q_base // (hkb * group) + hk, b * npair + i, 0))]
                 + kspecs + vspecs + [mspec, mspec] + bspecs,
        out_specs=pl.BlockSpec((2 * blk, hkb * group * LANES), lambda hk, b, i: (b * npair + i, hk)),
        out_shape=jax.ShapeDtypeStruct((tokens, hq * LANES), BF16),
        compiler_params=_cparams(("parallel", "parallel", "parallel"), ATTN_VMEM_MIB["swa"]),
        name="swa_attn",
    )(proj, *([proj] * 8), km, vm, bias, bias)


def _t5_bucket(rel):
    nb = T5_BUCKETS // 2
    max_exact = nb // 2
    ret = np.where(rel > 0, nb, 0)
    n = np.abs(rel)
    large = max_exact + (np.log(np.maximum(n, 1) / max_exact)
                         / math.log(T5_MAX_DIST / max_exact) * (nb - max_exact)).astype(np.int64)
    large = np.minimum(large, nb - 1)
    return ret + np.where(n < max_exact, n, large)


def _select_rows(table, idx):
    flat = idx.reshape(-1)
    onehot = np.zeros((table.shape[0], flat.size), np.float32)
    onehot[flat, np.arange(flat.size)] = 1.0
    out = jnp.einsum("bh,bx->xh", table.astype(F32), onehot, precision=lax.Precision.HIGHEST)
    return out.reshape(idx.shape + (table.shape[1],))


def _swa_tables(t5_bias, sink):
    blk = SWA_BLOCK
    rel = np.arange(-(2 * blk - 1), 2 * blk)
    by_rel = _select_rows(t5_bias, _t5_bucket(rel)).T
    by_rel = jnp.where((np.abs(rel) <= SWA_WINDOW)[None], by_rel, NEG_INF)
    bias_w = jnp.stack([by_rel[:, blk - 1 - q:blk - 1 - q + 3 * blk] for q in range(blk)], axis=1)
    jj = np.arange(3 * blk)
    edge = np.stack([np.zeros_like(jj, bool), jj < blk, jj >= 2 * blk, (jj < blk) | (jj >= 2 * blk)])
    bias_w = jnp.where(edge[:, None, None, :], NEG_INF, bias_w[None])
    assert N_META + blk - (N_META - 1) > T5_MAX_DIST
    rel_m = np.arange(N_META)[None, :] - (N_META + np.arange(2 * blk))[:, None]
    bias_m = _select_rows(t5_bias, _t5_bucket(rel_m))
    bias_m = bias_m.reshape(2, blk, N_META, -1).transpose(0, 3, 1, 2)
    bias_m = jnp.stack([bias_m[1], bias_m[0], bias_m[1], bias_m[0]])
    hq = bias_m.shape[1]
    sink_col = jnp.broadcast_to(sink.astype(F32)[None, :, None, None], (4, hq, blk, 1))
    pad = jnp.full((4, hq, blk, LANES - N_META - 1), NEG_INF, F32)
    return jnp.concatenate([bias_w, bias_m, sink_col, pad], axis=-1) * LOG2_E


def _trunk(x, meta_kv, tabs, params, later_weights):
    batch, seq, d = x.shape
    (g_attn, w_in, head_gain, head_flag, g_mlp) = params
    km_na, vm_na, km_sw, vm_sw = meta_kv
    na_tiles, na_plans, swa_bias = tabs
    nh_na = km_na.shape[0]
    hq_sw = swa_bias.shape[1]
    hkv_sw = km_sw.shape[0]
    x2 = x.reshape(batch * seq, d)
    casts = () if later_weights[0].dtype == BF16 else later_weights
    proj, *cast_out = _inproj(x2, g_attn, w_in, head_gain, head_flag, TILES["inproj"]["bm"], casts)
    w_out, w_up, w_down = later_weights = tuple(cast_out) if cast_out else later_weights
    rows = seq // GRID_W
    assert seq % (GRID_W * NA_QROWS) == 0 and rows >= NA_ROWS
    assert seq % SWA_BLOCK == 0
    na_o = _na_attention(proj, km_na, vm_na, na_tiles, na_plans, batch, rows)
    q_base = 3 * nh_na
    k_base = q_base + hq_sw
    v_base = k_base + hkv_sw
    sw_o = _swa_attention(proj, km_sw, vm_sw, swa_bias, batch, seq, q_base, k_base, v_base)
    x1, x1g, x1ss = _outproj(na_o, sw_o, w_out, x2, g_mlp)
    u = _mlp_up(x1g, x1ss, w_up)
    y = _mlp_down(u, w_down, x1)
    return y.reshape(batch, seq, d), later_weights


def kernel(x_prompt, x_sample, meta_tokens, t5_bias, norm_attn, w_in, q_norm_na, k_norm_na, na_rpb,
           q_norm_swa, k_norm_swa, swa_sink, w_out, norm_mlp, w_up, w_down):
    depth = w_in.shape[0]
    assert depth == 1
    nh_na = na_rpb.shape[1]
    hq_sw = swa_sink.shape[1]
    in_width = w_in.shape[2]
    hkv_sw = (in_width // HEAD_DIM - 3 * nh_na - hq_sw) // 2

    ones = jnp.ones((HEAD_DIM,), F32)

    def rep(g, n):
        return jnp.broadcast_to(g.astype(F32)[None], (n, HEAD_DIM))

    head_gain = jnp.concatenate([
        rep(q_norm_na[0], nh_na), rep(k_norm_na[0], nh_na), rep(ones, nh_na),
        rep(q_norm_swa[0], hq_sw), rep(k_norm_swa[0], hkv_sw), rep(ones, hkv_sw)])[:, None, :]
    flag = np.concatenate([np.ones(2 * nh_na), np.zeros(nh_na), np.ones(hq_sw + hkv_sw),
                           np.zeros(hkv_sw)]).astype(np.float32)
    head_flag = jnp.asarray(np.broadcast_to(flag[:, None, None], (flag.shape[0], 1, HEAD_DIM)))

    g_attn = norm_attn[0].astype(F32)[None]
    g_mlp = norm_mlp[0].astype(F32)[None]

    mproj, w_in_b = _meta_inproj(meta_tokens.astype(F32), g_attn, w_in[0].astype(F32),
                                 head_gain, head_flag)
    k0 = 3 * nh_na + hq_sw
    mpad = jnp.pad(mproj[k0:k0 + 2 * hkv_sw], ((0, 0), (0, LANES - N_META), (0, 0)))
    meta_kv = (mproj[nh_na:2 * nh_na], mproj[2 * nh_na:3 * nh_na], mpad[:hkv_sw], mpad[hkv_sw:])

    specs, plans = _na_plan()
    tabs = (_na_tiles(na_rpb[0], specs), plans, _swa_tables(t5_bias, swa_sink[0]))
    params = (g_attn, w_in_b, head_gain, head_flag, g_mlp)
    weights_f32 = (w_out[0].astype(F32), w_up[0].astype(F32), w_down[0].astype(F32))
    y_sample, weights_bf16 = _trunk(x_sample, meta_kv, tabs, params, weights_f32)
    y_prompt, _ = _trunk(x_prompt, meta_kv, tabs, params, weights_bf16)
    return (y_prompt, y_sample)
```
